```python
import math
import jax, jax.numpy as jnp
from jax import lax
import numpy as np

D_MODEL = 4096
BATCH = 8
SEQ = 2048
DEPTH = 2

D_MIX = D_MODEL
N_GROUPS = 4
GROUP_W = D_MIX // N_GROUPS
HEAD_DIM = 128

GLA_HEADS = 4
GLA_DV = GROUP_W // GLA_HEADS
GLA_DK = GLA_DV // 2
GLA_KEY = GLA_HEADS * GLA_DK
GLA_GATE_RANK = 16
GLA_GATE_NORMALIZER = 16.0
GLA_CHUNK = 64

SWA_HEADS = GROUP_W // HEAD_DIM
SWA_KV_HEADS = 2
SWA_KV = SWA_KV_HEADS * HEAD_DIM
SWA_WINDOW = 128
SWA_BLOCK = 128

MOBA_HEADS = GROUP_W // HEAD_DIM
MOBA_BLOCK = 256
MOBA_TOPK = 3
MOBA_Q_CHUNK = 16

SB_HEADS = GROUP_W // HEAD_DIM
SB_Q_BLOCK = 128

ROPE_THETA = 10000.0
D_FF = 11008
FFN_RES = 0.5
LN_EPS = 1e-5
RMS_EPS = 1e-5
DN_ALPHA = (2 * DEPTH) ** 0.25
DN_BETA = (8 * DEPTH) ** -0.25

IN_SPLIT_SIZES = (
    GLA_KEY, GLA_KEY, GROUP_W, GROUP_W, GLA_GATE_RANK,
    GROUP_W, SWA_KV, SWA_KV,
    GROUP_W, GROUP_W, GROUP_W,
    GROUP_W, GROUP_W, GROUP_W,
)
D_IN = sum(IN_SPLIT_SIZES)

kernel_name = "hybrid_parallel_heads_gla_swa_moba_stickbreak_macaron_deepnorm"


def layer_norm(x, g, b):
    xf = x.astype(jnp.float32)
    mu = jnp.mean(xf, axis=-1, keepdims=True)
    var = jnp.mean(jnp.square(xf - mu), axis=-1, keepdims=True)
    return ((xf - mu) * lax.rsqrt(var + LN_EPS) * g.astype(jnp.float32) + b.astype(jnp.float32)).astype(x.dtype)


def swiglu(h, w_gu, w_down):
    gate, up = jnp.split(h @ w_gu, 2, axis=-1)
    return (jax.nn.silu(gate) * up) @ w_down


def rope_tables(positions):
    inv = 1.0 / (ROPE_THETA ** (jnp.arange(0, HEAD_DIM, 2, dtype=jnp.float32) / HEAD_DIM))
    ang = positions.astype(jnp.float32)[..., None] * inv
    return jnp.cos(ang)[:, :, None, :], jnp.sin(ang)[:, :, None, :]


def apply_rope(x, cos, sin):
    xf = x.astype(jnp.float32)
    x1, x2 = jnp.split(xf, 2, axis=-1)
    return jnp.concatenate([x1 * cos - x2 * sin, x2 * cos + x1 * sin], axis=-1).astype(x.dtype)


def gla_mixer(q, k, v, g, gate_lr, w_gate_up, b_gate_up, norm_g):
    B, S, _ = q.shape
    f32 = jnp.float32
    C = GLA_CHUNK

    def chunks(t, d):
        return t.astype(f32).reshape(B, S // C, C, GLA_HEADS, d).transpose(1, 0, 3, 2, 4)

    log_decay = jax.nn.log_sigmoid((gate_lr @ w_gate_up + b_gate_up).astype(f32)) / GLA_GATE_NORMALIZER
    qc = chunks(q, GLA_DK) * (GLA_DK ** -0.5)
    kc = chunks(k, GLA_DK)
    vc = chunks(v, GLA_DV)
    gc = chunks(log_decay, GLA_DK)
    causal = jnp.tril(jnp.ones((C, C), dtype=bool))[None, None, :, :, None]

    def step(state, inp):
        q_, k_, v_, g_ = inp
        bcum = jnp.cumsum(g_, axis=2)
        b_last = bcum[:, :, -1:, :]
        decay = jnp.exp(jnp.where(causal, bcum[:, :, :, None, :] - bcum[:, :, None, :, :], -jnp.inf))
        att = jnp.einsum('bhid,bhjd,bhijd->bhij', q_, k_, decay)
        o = (jnp.einsum('bhij,bhje->bhie', att, v_)
             + jnp.einsum('bhid,bhde->bhie', q_ * jnp.exp(bcum), state))
        state = (jnp.exp(b_last[:, :, 0, :])[..., None] * state
                 + jnp.einsum('bhjd,bhje->bhde', k_ * jnp.exp(b_last - bcum), v_))
        return state, o

    state0 = jnp.zeros((B, GLA_HEADS, GLA_DK, GLA_DV), f32)
    _, o = lax.scan(step, state0, (qc, kc, vc, gc))
    o = o.transpose(1, 0, 3, 2, 4).reshape(B, S, GLA_HEADS, GLA_DV)
    o = o * lax.rsqrt(jnp.mean(jnp.square(o), axis=-1, keepdims=True) + RMS_EPS) * norm_g.astype(f32)
    o = o * jax.nn.silu(g.astype(f32).reshape(B, S, GLA_HEADS, GLA_DV))
    return o.reshape(B, S, GROUP_W)


def swa_mixer(q, k, v, sinks):
    B, S, Hq, d = q.shape
    Hkv = k.shape[2]
    G = Hq // Hkv
    NB = S // SWA_BLOCK
    qb = q.reshape(B, NB, SWA_BLOCK, Hkv, G, d)

    def band_keys(t):
        prev = jnp.concatenate([jnp.zeros_like(t[:, :SWA_BLOCK]), t[:, :S - SWA_BLOCK]], axis=1)
        return jnp.concatenate([prev.reshape(B, NB, SWA_BLOCK, Hkv, d),
                                t.reshape(B, NB, SWA_BLOCK, Hkv, d)], axis=2)

    kb, vb = band_keys(k), band_keys(v)
    scores = jnp.einsum('bnqkgd,bnskd->bnkgqs', qb, kb).astype(jnp.float32) * (d ** -0.5)
    qpos = SWA_BLOCK + jnp.arange(SWA_BLOCK)[:, None]
    kpos = jnp.arange(2 * SWA_BLOCK)[None, :]
    rel = qpos - kpos
    band = (rel >= 0) & (rel < SWA_WINDOW)
    first = (jnp.arange(NB) == 0)[:, None, None]
    mask = band[None] & ~(first & (kpos[None] < SWA_BLOCK))
    scores = jnp.where(mask[None, :, None, None], scores, -jnp.inf)
    sink = sinks.astype(jnp.float32).reshape(1, 1, Hkv, G, 1, 1)
    m = jnp.maximum(jnp.max(scores, axis=-1, keepdims=True), sink)
    p = jnp.exp(scores - m)
    probs = p / (jnp.sum(p, axis=-1, keepdims=True) + jnp.exp(sink - m))
    out = jnp.einsum('bnkgqs,bnskd->bnqkgd', probs.astype(v.dtype), vb)
    return out.reshape(B, S, Hq * d)


def moba_mixer(q, k, v):
    B, S, H, d = q.shape
    f32 = jnp.float32
    q, k, v = (t.transpose(0, 2, 1, 3) for t in (q, k, v))
    NB = -(-S // MOBA_BLOCK)
    pad = NB * MOBA_BLOCK - S
    kb = jnp.pad(k, ((0, 0), (0, 0), (0, pad), (0, 0))).reshape(B, H, NB, MOBA_BLOCK, d)
    vb = jnp.pad(v, ((0, 0), (0, 0), (0, pad), (0, 0))).reshape(B, H, NB, MOBA_BLOCK, d)
    kbar = jnp.mean(kb.astype(f32), axis=3)
    own = jnp.arange(S) // MOBA_BLOCK
    gate = jnp.einsum('bhsd,bhnd->bhsn', q.astype(f32), kbar)
    past = jnp.arange(NB)[None, :] < own[:, None]
    gate = jnp.where(past[None, None], gate, jnp.finfo(f32).min)
    ksel = min(MOBA_TOPK, NB)
    _, idx = lax.top_k(gate, ksel)
    sel_valid = jnp.arange(ksel)[None, :] < own[:, None]
    bi = jnp.arange(B)[:, None, None, None]
    hi = jnp.arange(H)[None, :, None, None]
    scale = d ** -0.5
    QC = MOBA_Q_CHUNK

    def chunk(c):
        t0 = c * QC
        qc = lax.dynamic_slice_in_dim(q, t0, QC, axis=2)
        ic = lax.dynamic_slice_in_dim(idx, t0, QC, axis=2)
        sm = lax.dynamic_slice_in_dim(sel_valid, t0, QC, axis=0)
        blk = t0 // MOBA_BLOCK
        k_sel = kb[bi, hi, ic]
        v_sel = vb[bi, hi, ic]
        s_sel = jnp.einsum('bhqd,bhqkpd->bhqkp', qc, k_sel).astype(f32) * scale
        s_sel = jnp.where(sm[None, None, :, :, None], s_sel, -jnp.inf)
        k_own = lax.dynamic_index_in_dim(kb, blk, axis=2, keepdims=False)
        v_own = lax.dynamic_index_in_dim(vb, blk, axis=2, keepdims=False)
        s_own = jnp.einsum('bhqd,bhpd->bhqp', qc, k_own).astype(f32) * scale
        tq = t0 + jnp.arange(QC)
        tk = blk * MOBA_BLOCK + jnp.arange(MOBA_BLOCK)
        s_own = jnp.where((tk[None, :] <= tq[:, None])[None, None], s_own, -jnp.inf)
        s_all = jnp.concatenate([s_sel.reshape(B, H, QC, ksel * MOBA_BLOCK), s_own], axis=-1)
        p = jax.nn.softmax(s_all, axis=-1).astype(v.dtype)
        p_sel = p[..., :ksel * MOBA_BLOCK].reshape(B, H, QC, ksel, MOBA_BLOCK)
        p_own = p[..., ksel * MOBA_BLOCK:]
        return (jnp.einsum('bhqkp,bhqkpd->bhqd', p_sel, v_sel)
                + jnp.einsum('bhqp,bhpd->bhqd', p_own, v_own))

    outs = lax.map(chunk, jnp.arange(S // QC))
    return outs.transpose(1, 0, 3, 2, 4).reshape(B, S, H * d)


def stick_breaking_mixer(q, k, v):
    B, S, H, d = q.shape
    f32 = jnp.float32
    q, k, v = (t.transpose(0, 2, 1, 3) for t in (q, k, v))
    scale = d ** -0.5
    outs = []
    for i in range(S // SB_Q_BLOCK):
        end = (i + 1) * SB_Q_BLOCK
        qb = q[:, :, i * SB_Q_BLOCK:end]
        kb, vb = k[:, :, :end], v[:, :, :end]
        z = jnp.einsum('bhqd,bhsd->bhqs', qb, kb).astype(f32) * scale
        tq = i * SB_Q_BLOCK + jnp.arange(SB_Q_BLOCK)
        ts = jnp.arange(end)
        mask = (ts[None, :] < tq[:, None])[None, None]
        log_keep = jnp.where(mask, jax.nn.log_sigmoid(-z), 0.0)
        log_after = lax.cumsum(log_keep, axis=3, reverse=True) - log_keep
        w = jnp.where(mask, jnp.exp(jax.nn.log_sigmoid(z) + log_after), 0.0)
        outs.append(jnp.einsum('bhqs,bhsd->bhqd', w.astype(v.dtype), vb))
    o = jnp.concatenate(outs, axis=2)
    return o.transpose(0, 2, 1, 3).reshape(B, S, H * d)


def token_mixing(h, cos, sin, w_in, gla_w_gate_up, gla_b_gate_up, gla_norm_g, swa_sinks, w_out):
    B, S, _ = h.shape
    points, acc = [], 0
    for size in IN_SPLIT_SIZES[:-1]:
        acc += size
        points.append(acc)
    (g_q, g_k, g_v, g_g, g_lr, s_q, s_k, s_v,
     m_q, m_k, m_v, b_q, b_k, b_v) = jnp.split(h @ w_in, points, axis=-1)

    y_gla = gla_mixer(g_q, g_k, g_v, g_g, g_lr, gla_w_gate_up, gla_b_gate_up, gla_norm_g)

    s_q = apply_rope(s_q.reshape(B, S, SWA_HEADS, HEAD_DIM), cos, sin)
    s_k = apply_rope(s_k.reshape(B, S, SWA_KV_HEADS, HEAD_DIM), cos, sin)
    y_swa = swa_mixer(s_q, s_k, s_v.reshape(B, S, SWA_KV_HEADS, HEAD_DIM), swa_sinks)

    m_q = apply_rope(m_q.reshape(B, S, MOBA_HEADS, HEAD_DIM), cos, sin)
    m_k = apply_rope(m_k.reshape(B, S, MOBA_HEADS, HEAD_DIM), cos, sin)
    y_moba = moba_mixer(m_q, m_k, m_v.reshape(B, S, MOBA_HEADS, HEAD_DIM))

    y_sb = stick_breaking_mixer(b_q.reshape(B, S, SB_HEADS, HEAD_DIM),
                                b_k.reshape(B, S, SB_HEADS, HEAD_DIM),
                                b_v.reshape(B, S, SB_HEADS, HEAD_DIM))

    y = jnp.concatenate([y_gla.astype(h.dtype), y_swa.astype(h.dtype),
                         y_moba.astype(h.dtype), y_sb.astype(h.dtype)], axis=-1)
    return y @ w_out


def setup_inputs(seed: int = 0) -> dict:
    key = jax.random.key(seed)
    ks = jax.random.split(key, 15)
    f32 = jnp.float32
    L = DEPTH
    nrm = lambda k, shape: jax.random.normal(k, shape, f32)
    x = nrm(ks[0], (BATCH, SEQ, D_MODEL))
    offsets = jax.random.randint(ks[1], (BATCH, 1), 0, 4096, dtype=jnp.int32)
    positions = (offsets + jnp.arange(SEQ, dtype=jnp.int32)[None, :]).astype(jnp.int32)
    w_in = nrm(ks[2], (L, D_MODEL, D_IN)) * D_MODEL ** -0.5
    gla_w_gate_up = nrm(ks[3], (L, GLA_GATE_RANK, GLA_KEY)) * GLA_GATE_RANK ** -0.5
    gla_b_gate_up = 0.1 * nrm(ks[4], (L, GLA_KEY))
    gla_norm_g = 1.0 + 0.02 * nrm(ks[5], (L, GLA_DV))
    swa_sinks = 0.5 * nrm(ks[6], (L, SWA_HEADS))
    w_out = nrm(ks[7], (L, D_MIX, D_MODEL)) * (D_MIX ** -0.5) * DN_BETA
    ffn1_w_gu = nrm(ks[8], (L, D_MODEL, 2 * D_FF)) * D_MODEL ** -0.5
    ffn1_w_down = nrm(ks[9], (L, D_FF, D_MODEL)) * (D_FF ** -0.5) * DN_BETA
    ffn2_w_gu = nrm(ks[10], (L, D_MODEL, 2 * D_FF)) * D_MODEL ** -0.5
    ffn2_w_down = nrm(ks[11], (L, D_FF, D_MODEL)) * (D_FF ** -0.5) * DN_BETA
    ln_g = 1.0 + 0.02 * nrm(ks[12], (L, 3, D_MODEL))
    ln_b = 0.02 * nrm(ks[13], (L, 3, D_MODEL))
    return {"x": x, "positions": positions, "w_in": w_in,
            "gla_w_gate_up": gla_w_gate_up, "gla_b_gate_up": gla_b_gate_up,
            "gla_norm_g": gla_norm_g, "swa_sinks": swa_sinks, "w_out": w_out,
            "ffn1_w_gu": ffn1_w_gu, "ffn1_w_down": ffn1_w_down,
            "ffn2_w_gu": ffn2_w_gu, "ffn2_w_down": ffn2_w_down,
            "ln_g": ln_g, "ln_b": ln_b}


def reference(x, positions, w_in, gla_w_gate_up, gla_b_gate_up, gla_norm_g, swa_sinks, w_out,
              ffn1_w_gu, ffn1_w_down, ffn2_w_gu, ffn2_w_down, ln_g, ln_b):
    cos, sin = rope_tables(positions)
    for l in range(DEPTH):
        x = layer_norm(DN_ALPHA * x + FFN_RES * swiglu(x, ffn1_w_gu[l], ffn1_w_down[l]),
                       ln_g[l, 0], ln_b[l, 0])
        x = layer_norm(DN_ALPHA * x + token_mixing(x, cos, sin, w_in[l], gla_w_gate_up[l],
                                                   gla_b_gate_up[l], gla_norm_g[l],
                                                   swa_sinks[l], w_out[l]),
                       ln_g[l, 1], ln_b[l, 1])
        x = layer_norm(DN_ALPHA * x + FFN_RES * swiglu(x, ffn2_w_gu[l], ffn2_w_down[l]),
                       ln_g[l, 2], ln_b[l, 2])
    return x
```

```python
import functools

import jax
import jax.numpy as jnp
from jax import lax
from jax.experimental import pallas as pl
from jax.experimental.pallas import tpu as pltpu

F32 = jnp.float32
BF16 = jnp.bfloat16

D_MODEL = 4096
DEPTH = 2
GROUP_W = 1024
HEAD_DIM = 128
GLA_HEADS = 4
GLA_DV = 256
GLA_DK = 128
GLA_KEY = 512
GLA_GATE_RANK = 16
GLA_GATE_NORMALIZER = 16.0
GLA_CHUNK = 64
SWA_HEADS = 8
SWA_KV_HEADS = 2
SWA_GROUP = SWA_HEADS // SWA_KV_HEADS
SWA_WINDOW = 128
SWA_BLOCK = 128
MOBA_HEADS = 8
MOBA_BLOCK = 256
MOBA_TOPK = 3
SB_HEADS = 8
ROPE_THETA = 10000.0
D_FF = 11008
FFN_RES = 0.5
LN_EPS = 1e-5
RMS_EPS = 1e-5
DN_ALPHA = (2 * DEPTH) ** 0.25

LANES = 128
VMEM_LIMIT = 56 * 1024 * 1024

COL_GQ, COL_GK, COL_GV, COL_GG = 0, 512, 1024, 2048
COL_SQ, COL_SK, COL_SV = 3072, 4096, 4352
COL_MQ, COL_MK, COL_MV = 4608, 5632, 6656
COL_BQ, COL_BK, COL_BV = 7680, 8704, 9728
COL_LR = 10752
PROJ_TN = 1024
NP = 11264

NEG_INF = float("-inf")


def _params(sem, vmem=VMEM_LIMIT):
    return pltpu.CompilerParams(dimension_semantics=sem, vmem_limit_bytes=vmem)


def _layer_norm_rows(y, g, b):
    mu = jnp.mean(y, axis=-1, keepdims=True)
    yc = y - mu
    var = jnp.mean(yc * yc, axis=-1, keepdims=True)
    return yc * lax.rsqrt(var + LN_EPS) * g + b


LN_ROWS = 32
ACC_COLS = 512


def _residual_layer_norm(o_ref, x_ref, res_scale, g_ref, b_ref):
    def body(r, carry):
        rows = pl.ds(pl.multiple_of(r * LN_ROWS, LN_ROWS), LN_ROWS)
        y = o_ref[rows, :]
        if x_ref is not None:
            y = DN_ALPHA * x_ref[rows, :] + res_scale * y
        o_ref[rows, :] = _layer_norm_rows(y, g_ref[...], b_ref[...])
        return carry

    lax.fori_loop(0, o_ref.shape[0] // LN_ROWS, body, 0)


def _accumulate_dot(o_ref, a, w_ref):
    for c in range(0, o_ref.shape[1], ACC_COLS):
        o_ref[:, c:c + ACC_COLS] += jnp.dot(a, w_ref[:, c:c + ACC_COLS], preferred_element_type=F32)


def _rope_kernel(pos_ref, inv_ref, sign_ref, cos_ref, sin_ref):
    ang = pos_ref[0].astype(F32) * inv_ref[...]
    cos_ref[0] = jnp.cos(ang)
    sin_ref[0] = jnp.sin(ang) * sign_ref[...]


def rope_tables(positions):
    B, S = positions.shape
    ts = min(S, 512)
    inv = 1.0 / (ROPE_THETA ** (jnp.arange(0, HEAD_DIM, 2, dtype=F32) / HEAD_DIM))
    inv_full = jnp.concatenate([inv, inv]).reshape(1, HEAD_DIM)
    sign = jnp.concatenate([-jnp.ones((HEAD_DIM // 2,), F32), jnp.ones((HEAD_DIM // 2,), F32)]).reshape(1, HEAD_DIM)
    pos_b = jnp.broadcast_to(positions[:, :, None], (B, S, HEAD_DIM))
    blk = pl.BlockSpec((1, ts, HEAD_DIM), lambda b, s: (b, s, 0))
    vec = pl.BlockSpec((1, HEAD_DIM), lambda b, s: (0, 0))
    return pl.pallas_call(
        _rope_kernel,
        out_shape=(jax.ShapeDtypeStruct((B, S, HEAD_DIM), F32),) * 2,
        grid=(B, S // ts),
        in_specs=[blk, vec, vec],
        out_specs=(blk, blk),
        compiler_params=_params(("parallel", "parallel")),
        name="rope_tables",
    )(pos_b, inv_full, sign)


def _rope(x, cos, sin):
    return x * cos + pltpu.roll(x, HEAD_DIM // 2, axis=1) * sin


FFN_TM = 512
FFN_TF = 256


def _ffn_kernel(x_ref, wg_ref, wu_ref, wd_ref, g_ref, b_ref, o_ref, xb_ref):
    j = pl.program_id(1)

    @pl.when(j == 0)
    def _():
        xb_ref[...] = x_ref[...].astype(BF16)
        o_ref[...] = jnp.zeros_like(o_ref)

    xb = xb_ref[...]
    gate = jnp.dot(xb, wg_ref[...], preferred_element_type=F32)
    up = jnp.dot(xb, wu_ref[...], preferred_element_type=F32)
    act = (gate / (1.0 + jnp.exp(-gate)) * up).astype(BF16)
    _accumulate_dot(o_ref, act, wd_ref)

    @pl.when(j == pl.num_programs(1) - 1)
    def _():
        _residual_layer_norm(o_ref, x_ref, FFN_RES, g_ref, b_ref)


def ffn_sublayer(x, w_gu, w_down, ln_g, ln_b):
    T, D = x.shape
    tm = min(FFN_TM, T)
    nj = D_FF // FFN_TF
    vec = pl.BlockSpec((1, D), lambda i, j: (0, 0))
    return pl.pallas_call(
        _ffn_kernel,
        out_shape=jax.ShapeDtypeStruct((T, D), F32),
        grid=(T // tm, nj),
        in_specs=[
            pl.BlockSpec((tm, D), lambda i, j: (i, 0)),
            pl.BlockSpec((D, FFN_TF), lambda i, j: (0, j)),
            pl.BlockSpec((D, FFN_TF), lambda i, j: (0, j + nj)),
            pl.BlockSpec((FFN_TF, D), lambda i, j: (j, 0)),
            vec, vec,
        ],
        out_specs=pl.BlockSpec((tm, D), lambda i, j: (i, 0)),
        scratch_shapes=[pltpu.VMEM((tm, D), BF16)],
        compiler_params=_params(("parallel", "arbitrary")),
        name="ffn",
    )(x, w_gu, w_gu, w_down, ln_g, ln_b)


PROJ_TM = 512


def _proj_kernel(x_ref, w_ref, o_ref, xb_ref):
    @pl.when(pl.program_id(1) == 0)
    def _():
        xb_ref[...] = x_ref[...].astype(BF16)

    o_ref[...] = jnp.dot(xb_ref[...], w_ref[...], preferred_element_type=F32)


def in_projection(x, w_in_r):
    T, D = x.shape
    tm = min(PROJ_TM, T)
    return pl.pallas_call(
        _proj_kernel,
        out_shape=jax.ShapeDtypeStruct((T, NP), F32),
        grid=(T // tm, NP // PROJ_TN),
        in_specs=[
            pl.BlockSpec((tm, D), lambda i, j: (i, 0)),
            pl.BlockSpec((D, PROJ_TN), lambda i, j: (0, j)),
        ],
        out_specs=pl.BlockSpec((tm, PROJ_TN), lambda i, j: (i, j)),
        scratch_shapes=[pltpu.VMEM((tm, D), BF16)],
        compiler_params=_params(("parallel", "arbitrary")),
        name="in_proj",
    )(x, w_in_r)


OUT_TM = 256


def _outproj_kernel(x_ref, y0_ref, y1_ref, y2_ref, y3_ref, w_ref, g_ref, b_ref, o_ref):
    p = pl.program_id(1)

    @pl.when(p == 0)
    def _():
        o_ref[...] = DN_ALPHA * x_ref[...]

    for idx, y_ref in enumerate((y0_ref, y1_ref, y2_ref, y3_ref)):
        @pl.when(p == idx)
        def _(y_ref=y_ref):
            _accumulate_dot(o_ref, y_ref[...], w_ref)

    @pl.when(p == pl.num_programs(1) - 1)
    def _():
        _residual_layer_norm(o_ref, None, 1.0, g_ref, b_ref)


def out_projection(x, ys, w_out, ln_g, ln_b):
    T, D = x.shape
    tm = min(OUT_TM, T)
    vec = pl.BlockSpec((1, D), lambda i, p: (0, 0))
    yspec = pl.BlockSpec((tm, GROUP_W), lambda i, p: (i, 0))
    return pl.pallas_call(
        _outproj_kernel,
        out_shape=jax.ShapeDtypeStruct((T, D), F32),
        grid=(T // tm, 4),
        in_specs=[
            pl.BlockSpec((tm, D), lambda i, p: (i, 0)),
            yspec, yspec, yspec, yspec,
            pl.BlockSpec((GROUP_W, D), lambda i, p: (p, 0)),
            vec, vec,
        ],
        out_specs=pl.BlockSpec((tm, D), lambda i, p: (i, 0)),
        compiler_params=_params(("parallel", "arbitrary")),
        name="out_proj",
    )(x, *ys, w_out, ln_g, ln_b)


GLA_ROWS = 512
SUBLANES = 8


def _cumsum_rows(x):
    n = x.shape[0]
    row = lax.broadcasted_iota(jnp.int32, x.shape, 0)
    sh = 1
    while sh < n:
        x = x + jnp.where(row >= sh, pltpu.roll(x, sh, axis=0), 0.0)
        sh *= 2
    return x


def _gla_kernel(q_ref, k_ref, v_ref, gg_ref, lr_ref, wg_ref, bg_ref, ng_ref, o_ref,
                st_ref, b_scr, k_scr, v_scr):
    C = GLA_CHUNK

    @pl.when(pl.program_id(2) == 0)
    def _():
        st_ref[...] = jnp.zeros_like(st_ref)

    n_chunks = q_ref.shape[1] // C
    row8 = lax.broadcasted_iota(jnp.int32, (SUBLANES, 1), 0)

    def chunk(c, carry):
        r0 = pl.multiple_of(c * C, C)
        rows = pl.ds(r0, C)
        q = q_ref[0, rows, :] * (GLA_DK ** -0.5)
        k = k_ref[0, rows, :]
        v = v_ref[0, rows, :]
        logits = jnp.dot(lr_ref[0, rows, :], wg_ref[...], preferred_element_type=F32,
                         precision=lax.Precision.HIGHEST) + bg_ref[...]
        g = (jnp.minimum(logits, 0.0) - jnp.log(1.0 + jnp.exp(-jnp.abs(logits)))) / GLA_GATE_NORMALIZER
        b = _cumsum_rows(g)
        b_last = b[C - 1:C, :]
        b_scr[...] = b
        k_scr[...] = k
        v_scr[...] = v

        st = st_ref[...]
        o_inter = lax.dot_general((q * jnp.exp(b)).astype(BF16), st.astype(BF16),
                                  (((1,), (1,)), ((), ())), preferred_element_type=F32)
        kd = (k * jnp.exp(b_last - b)).astype(BF16)
        st_ref[...] = st * jnp.exp(b_last) + lax.dot_general(
            v.astype(BF16), kd, (((0,), (0,)), ((), ())), preferred_element_type=F32)

        for ib in range(C // SUBLANES):
            i0 = ib * SUBLANES
            q_i = q[i0:i0 + SUBLANES]
            b_i = b[i0:i0 + SUBLANES]
            acc = o_inter[i0:i0 + SUBLANES]
            for j in range(i0 + SUBLANES):
                b_j = b_scr[j:j + 1, :]
                k_j = k_scr[j:j + 1, :]
                v_j = v_scr[j:j + 1, :]
                e = jnp.exp(jnp.minimum(b_i - b_j, 0.0))
                a = jnp.sum(q_i * k_j * e, axis=-1, keepdims=True)
                if j >= i0:
                    a = jnp.where(row8 >= (j - i0), a, 0.0)
                acc = acc + a * v_j
            acc = acc * lax.rsqrt(jnp.mean(acc * acc, axis=-1, keepdims=True) + RMS_EPS) * ng_ref[...]
            gg = gg_ref[0, pl.ds(r0 + i0, SUBLANES), :]
            o_ref[0, pl.ds(r0 + i0, SUBLANES), :] = (acc * (gg / (1.0 + jnp.exp(-gg)))).astype(o_ref.dtype)
        return carry

    lax.fori_loop(0, n_chunks, chunk, 0)


def gla_mixer(P, wg_pad, bg, ng):
    B, S, _ = P.shape
    R = min(GLA_ROWS, S)
    kb = lambda base: (lambda b, h, r: (b, r, base // GLA_DK + h))
    vb = lambda base: (lambda b, h, r: (b, r, base // GLA_DV + h))
    return pl.pallas_call(
        _gla_kernel,
        out_shape=jax.ShapeDtypeStruct((B, S, GROUP_W), BF16),
        grid=(B, GLA_HEADS, S // R),
        in_specs=[
            pl.BlockSpec((1, R, GLA_DK), kb(COL_GQ)),
            pl.BlockSpec((1, R, GLA_DK), kb(COL_GK)),
            pl.BlockSpec((1, R, GLA_DV), vb(COL_GV)),
            pl.BlockSpec((1, R, GLA_DV), vb(COL_GG)),
            pl.BlockSpec((1, R, LANES), lambda b, h, r: (b, r, COL_LR // LANES)),
            pl.BlockSpec((LANES, GLA_DK), lambda b, h, r: (0, h)),
            pl.BlockSpec((1, GLA_DK), lambda b, h, r: (0, h)),
            pl.BlockSpec((1, GLA_DV), lambda b, h, r: (0, 0)),
        ],
        out_specs=pl.BlockSpec((1, R, GLA_DV), lambda b, h, r: (b, r, h)),
        scratch_shapes=[
            pltpu.VMEM((GLA_DV, GLA_DK), F32),
            pltpu.VMEM((GLA_CHUNK, GLA_DK), F32),
            pltpu.VMEM((GLA_CHUNK, GLA_DK), F32),
            pltpu.VMEM((GLA_CHUNK, GLA_DV), F32),
        ],
        compiler_params=_params(("parallel", "parallel", "arbitrary")),
        name="gla",
    )(P, P, P, P, P, wg_pad, bg, ng)


def _swa_kernel(sink_ref, q_ref, kc_ref, kp_ref, vc_ref, vp_ref, cc_ref, sc_ref, cp_ref, sp_ref, o_ref):
    kv = pl.program_id(1)
    n = pl.program_id(2)
    BLK = SWA_BLOCK
    cos, sin = cc_ref[0], sc_ref[0]
    k_cat = jnp.concatenate([_rope(kp_ref[0], cp_ref[0], sp_ref[0]), _rope(kc_ref[0], cos, sin)],
                            axis=0).astype(BF16)
    v_cat = jnp.concatenate([vp_ref[0], vc_ref[0]], axis=0).astype(BF16)
    qpos = BLK + lax.broadcasted_iota(jnp.int32, (BLK, 2 * BLK), 0)
    kpos = lax.broadcasted_iota(jnp.int32, (BLK, 2 * BLK), 1)
    rel = qpos - kpos
    mask = (rel >= 0) & (rel < SWA_WINDOW) & ((kpos >= BLK) | (n > 0))
    for g in range(SWA_GROUP):
        cols = slice(g * HEAD_DIM, (g + 1) * HEAD_DIM)
        q = _rope(q_ref[0, :, cols], cos, sin).astype(BF16)
        s = lax.dot_general(q, k_cat, (((1,), (1,)), ((), ())), preferred_element_type=F32) * (HEAD_DIM ** -0.5)
        s = jnp.where(mask, s, NEG_INF)
        sink = sink_ref[kv * SWA_GROUP + g]
        m = jnp.maximum(jnp.max(s, axis=-1, keepdims=True), sink)
        p = jnp.exp(s - m)
        den = jnp.sum(p, axis=-1, keepdims=True) + jnp.exp(sink - m)
        o = jnp.dot(p.astype(BF16), v_cat, preferred_element_type=F32) / den
        o_ref[0, :, cols] = o.astype(o_ref.dtype)


def swa_mixer(P, cos, sin, sinks):
    B, S, _ = P.shape
    BLK = SWA_BLOCK
    cur = lambda base: (lambda b, kv, n, s: (b, n, base // HEAD_DIM + kv))
    prev = lambda base: (lambda b, kv, n, s: (b, jnp.maximum(n - 1, 0), base // HEAD_DIM + kv))
    hd = lambda f: pl.BlockSpec((1, BLK, HEAD_DIM), f)
    qw = SWA_GROUP * HEAD_DIM
    return pl.pallas_call(
        _swa_kernel,
        out_shape=jax.ShapeDtypeStruct((B, S, GROUP_W), BF16),
        grid_spec=pltpu.PrefetchScalarGridSpec(
            num_scalar_prefetch=1,
            grid=(B, SWA_KV_HEADS, S // BLK),
            in_specs=[
                pl.BlockSpec((1, BLK, qw), lambda b, kv, n, s: (b, n, COL_SQ // qw + kv)),
                hd(cur(COL_SK)), hd(prev(COL_SK)), hd(cur(COL_SV)), hd(prev(COL_SV)),
                hd(lambda b, kv, n, s: (b, n, 0)), hd(lambda b, kv, n, s: (b, n, 0)),
                hd(lambda b, kv, n, s: (b, jnp.maximum(n - 1, 0), 0)),
                hd(lambda b, kv, n, s: (b, jnp.maximum(n - 1, 0), 0)),
            ],
            out_specs=pl.BlockSpec((1, BLK, qw), lambda b, kv, n, s: (b, n, kv)),
        ),
        compiler_params=_params(("parallel", "parallel", "arbitrary")),
        name="swa",
    )(sinks, P, P, P, P, P, cos, sin, cos, sin)


def _moba_kernel(q_ref, k_ref, v_ref, cos_ref, sin_ref, o_ref, kr_ref, kbar_ref):
    i = pl.program_id(2)
    BLK = MOBA_BLOCK
    S = k_ref.shape[1]
    NB = S // BLK

    @pl.when(i == 0)
    def _():
        kr = _rope(k_ref[0], cos_ref[0], sin_ref[0])
        kr_ref[...] = kr.astype(BF16)
        for n in range(NB):
            kbar_ref[n:n + 1, :] = jnp.mean(kr[n * BLK:(n + 1) * BLK], axis=0, keepdims=True)

    rows = pl.ds(pl.multiple_of(i * BLK, BLK), BLK)
    q = _rope(q_ref[0], cos_ref[0, rows, :], sin_ref[0, rows, :])

    gates = [jnp.sum(q * kbar_ref[n:n + 1, :], axis=-1, keepdims=True) for n in range(NB)]
    sel = []
    for n in range(NB):
        rank = jnp.zeros_like(gates[n])
        for m in range(NB):
            if m == n:
                continue
            beats = (gates[m] >= gates[n]) if m < n else (gates[m] > gates[n])
            rank = rank + jnp.where(beats & (m < i), 1.0, 0.0)
        sel.append((rank < float(MOBA_TOPK)) & (n < i))

    qb = q.astype(BF16)
    tq = lax.broadcasted_iota(jnp.int32, (BLK, BLK), 0)
    tk = lax.broadcasted_iota(jnp.int32, (BLK, BLK), 1)
    causal = tk <= tq
    pieces = []
    for n in range(NB):
        s = lax.dot_general(qb, kr_ref[n * BLK:(n + 1) * BLK, :], (((1,), (1,)), ((), ())),
                            preferred_element_type=F32) * (HEAD_DIM ** -0.5)
        allow = sel[n] | (causal & (n == i))
        pieces.append(jnp.where(allow, s, NEG_INF))
    m_row = pieces[0].max(axis=-1, keepdims=True)
    for n in range(1, NB):
        m_row = jnp.maximum(m_row, pieces[n].max(axis=-1, keepdims=True))
    den = jnp.zeros_like(m_row)
    acc = jnp.zeros((BLK, HEAD_DIM), F32)
    for n in range(NB):
        p = jnp.exp(pieces[n] - m_row)
        den = den + jnp.sum(p, axis=-1, keepdims=True)
        acc = acc + jnp.dot(p.astype(BF16), v_ref[0, n * BLK:(n + 1) * BLK, :].astype(BF16),
                            preferred_element_type=F32)
    o_ref[0] = (acc / den).astype(o_ref.dtype)


def moba_mixer(P, cos, sin):
    B, S, _ = P.shape
    BLK = MOBA_BLOCK
    full = lambda base: pl.BlockSpec((1, S, HEAD_DIM), lambda b, h, i: (b, 0, base // HEAD_DIM + h))
    tab = pl.BlockSpec((1, S, HEAD_DIM), lambda b, h, i: (b, 0, 0))
    return pl.pallas_call(
        _moba_kernel,
        out_shape=jax.ShapeDtypeStruct((B, S, GROUP_W), BF16),
        grid=(B, MOBA_HEADS, S // BLK),
        in_specs=[
            pl.BlockSpec((1, BLK, HEAD_DIM), lambda b, h, i: (b, i, COL_MQ // HEAD_DIM + h)),
            full(COL_MK), full(COL_MV), tab, tab,
        ],
        out_specs=pl.BlockSpec((1, BLK, HEAD_DIM), lambda b, h, i: (b, i, h)),
        scratch_shapes=[pltpu.VMEM((S, HEAD_DIM), BF16), pltpu.VMEM((S // BLK, HEAD_DIM), F32)],
        compiler_params=_params(("parallel", "parallel", "arbitrary")),
        name="moba",
    )(P, P, P, cos, sin)


SB_TILE = 256


def _sb_kernel(q_ref, k_ref, v_ref, o_ref):
    i = pl.program_id(2)
    T = SB_TILE
    qb = q_ref[0].astype(BF16)
    srow = lax.broadcasted_iota(jnp.int32, (T, T), 0)
    scol = lax.broadcasted_iota(jnp.int32, (T, T), 1)
    upper = jnp.where(srow > scol, 1.0, 0.0).astype(BF16)
    strict = scol < srow

    def tile(j, run, acc, diagonal):
        rows = pl.ds(pl.multiple_of(j * T, T), T)
        kb = k_ref[0, rows, :].astype(BF16)
        vb = v_ref[0, rows, :].astype(BF16)
        z = lax.dot_general(qb, kb, (((1,), (1,)), ((), ())), preferred_element_type=F32) * (HEAD_DIM ** -0.5)
        soft = jnp.log(1.0 + jnp.exp(-jnp.abs(z)))
        log_beta = jnp.minimum(z, 0.0) - soft
        log_keep = jnp.minimum(-z, 0.0) - soft
        if diagonal:
            log_keep = jnp.where(strict, log_keep, 0.0)
        hi = log_keep.astype(BF16)
        lo = (log_keep - hi.astype(F32)).astype(BF16)
        after = (jnp.dot(hi, upper, preferred_element_type=F32)
                 + jnp.dot(lo, upper, preferred_element_type=F32)) + run
        w = jnp.exp(log_beta + after)
        if diagonal:
            w = jnp.where(strict, w, 0.0)
        acc = acc + jnp.dot(w.astype(BF16), vb, preferred_element_type=F32)
        run = run + jnp.sum(log_keep, axis=-1, keepdims=True)
        return run, acc

    run, acc = tile(i, jnp.zeros((T, 1), F32), jnp.zeros((T, HEAD_DIM), F32), True)

    def body(t, carry):
        return tile(i - 1 - t, carry[0], carry[1], False)

    run, acc = lax.fori_loop(0, i, body, (run, acc))
    o_ref[0] = acc.astype(o_ref.dtype)


def sb_mixer(P):
    B, S, _ = P.shape
    T = SB_TILE
    full = lambda base: pl.BlockSpec((1, S, HEAD_DIM), lambda b, h, i: (b, 0, base // HEAD_DIM + h))
    return pl.pallas_call(
        _sb_kernel,
        out_shape=jax.ShapeDtypeStruct((B, S, GROUP_W), BF16),
        grid=(B, SB_HEADS, S // T),
        in_specs=[
            pl.BlockSpec((1, T, HEAD_DIM), lambda b, h, i: (b, i, COL_BQ // HEAD_DIM + h)),
            full(COL_BK), full(COL_BV),
        ],
        out_specs=pl.BlockSpec((1, T, HEAD_DIM), lambda b, h, i: (b, i, h)),
        compiler_params=_params(("parallel", "parallel", "arbitrary")),
        name="stick_breaking",
    )(P, P, P)


def _relayout_w_in(w):
    lr0 = COL_GG + GROUP_W
    lr1 = lr0 + GLA_GATE_RANK
    pad = jnp.zeros((w.shape[0], NP - w.shape[1]), w.dtype)
    return jnp.concatenate([w[:, :lr0], w[:, lr1:], w[:, lr0:lr1], pad], axis=1).astype(BF16)


def token_mixing(x2, B, S, cos, sin, w_in_r, wg_pad, bg, ng, sinks, w_out_b, ln_g, ln_b):
    P = in_projection(x2, w_in_r).reshape(B, S, NP)
    ys = (gla_mixer(P, wg_pad, bg, ng), swa_mixer(P, cos, sin, sinks), moba_mixer(P, cos, sin), sb_mixer(P))
    ys = [y.reshape(B * S, GROUP_W) for y in ys]
    return out_projection(x2, ys, w_out_b, ln_g, ln_b)


def kernel(x, positions, w_in, gla_w_gate_up, gla_b_gate_up, gla_norm_g, swa_sinks, w_out,
           ffn1_w_gu, ffn1_w_down, ffn2_w_gu, ffn2_w_down, ln_g, ln_b):
    B, S, D = x.shape
    cos, sin = rope_tables(positions)
    x2 = x.reshape(B * S, D)
    for l in range(DEPTH):
        g = ln_g[l].reshape(3, 1, D)
        b = ln_b[l].reshape(3, 1, D)
        x2 = ffn_sublayer(x2, ffn1_w_gu[l].astype(BF16), ffn1_w_down[l].astype(BF16), g[0], b[0])
        wg_pad = jnp.zeros((LANES, GLA_KEY), F32).at[:GLA_GATE_RANK].set(gla_w_gate_up[l])
        x2 = token_mixing(x2, B, S, cos, sin, _relayout_w_in(w_in[l]), wg_pad,
                          gla_b_gate_up[l].reshape(1, GLA_KEY), gla_norm_g[l].reshape(1, GLA_DV),
                          swa_sinks[l], w_out[l].astype(BF16), g[1], b[1])
        x2 = ffn_sublayer(x2, ffn2_w_gu[l].astype(BF16), ffn2_w_down[l].astype(BF16), g[2], b[2])
    return x2.reshape(B, S, D)
```

```python
import functools

import jax
import jax.numpy as jnp
from jax import lax
from jax.experimental import pallas as pl
from jax.experimental.pallas import tpu as pltpu

F32 = jnp.float32
BF16 = jnp.bfloat16

D_MODEL = 4096
DEPTH = 2
GROUP_W = 1024
HEAD_DIM = 128
GLA_HEADS = 4
GLA_DV = 256
GLA_DK = 128
GLA_KEY = 512
GLA_GATE_RANK = 16
GLA_GATE_NORMALIZER = 16.0
GLA_CHUNK = 64
SWA_HEADS = 8
SWA_KV_HEADS = 2
SWA_GROUP = SWA_HEADS // SWA_KV_HEADS
SWA_WINDOW = 128
SWA_BLOCK = 128
MOBA_HEADS = 8
MOBA_BLOCK = 256
MOBA_TOPK = 3
SB_HEADS = 8
ROPE_THETA = 10000.0
D_FF = 11008
FFN_RES = 0.5
LN_EPS = 1e-5
RMS_EPS = 1e-5
DN_ALPHA = (2 * DEPTH) ** 0.25

LANES = 128
VMEM_LIMIT = 56 * 1024 * 1024

COL_GQ, COL_GK, COL_GV, COL_GG = 0, 512, 1024, 2048
COL_SQ, COL_SK, COL_SV = 3072, 4096, 4352
COL_MQ, COL_MK, COL_MV = 4608, 5632, 6656
COL_BQ, COL_BK, COL_BV = 7680, 8704, 9728
COL_LR = 10752
PROJ_TN = 1024
NP = 11264

NEG_INF = float("-inf")


def _params(sem, vmem=VMEM_LIMIT):
    return pltpu.CompilerParams(dimension_semantics=sem, vmem_limit_bytes=vmem)


def _layer_norm_rows(y, g, b):
    mu = jnp.mean(y, axis=-1, keepdims=True)
    yc = y - mu
    var = jnp.mean(yc * yc, axis=-1, keepdims=True)
    return yc * lax.rsqrt(var + LN_EPS) * g + b


LN_ROWS = 32
ACC_COLS = 512


def _residual_layer_norm(o_ref, x_ref, res_scale, g_ref, b_ref):
    def body(r, carry):
        rows = pl.ds(pl.multiple_of(r * LN_ROWS, LN_ROWS), LN_ROWS)
        y = o_ref[rows, :]
        if x_ref is not None:
            y = DN_ALPHA * x_ref[rows, :] + res_scale * y
        o_ref[rows, :] = _layer_norm_rows(y, g_ref[...], b_ref[...])
        return carry

    lax.fori_loop(0, o_ref.shape[0] // LN_ROWS, body, 0)


def _accumulate_dot(o_ref, a, w_ref):
    for c in range(0, o_ref.shape[1], ACC_COLS):
        o_ref[:, c:c + ACC_COLS] += jnp.dot(a, w_ref[:, c:c + ACC_COLS], preferred_element_type=F32)


def _rope_kernel(pos_ref, inv_ref, sign_ref, cos_ref, sin_ref):
    ang = pos_ref[0].astype(F32) * inv_ref[...]
    cos_ref[0] = jnp.cos(ang)
    sin_ref[0] = jnp.sin(ang) * sign_ref[...]


def rope_tables(positions):
    B, S = positions.shape
    ts = min(S, 512)
    inv = 1.0 / (ROPE_THETA ** (jnp.arange(0, HEAD_DIM, 2, dtype=F32) / HEAD_DIM))
    inv_full = jnp.concatenate([inv, inv]).reshape(1, HEAD_DIM)
    sign = jnp.concatenate([-jnp.ones((HEAD_DIM // 2,), F32), jnp.ones((HEAD_DIM // 2,), F32)]).reshape(1, HEAD_DIM)
    pos_b = jnp.broadcast_to(positions[:, :, None], (B, S, HEAD_DIM))
    blk = pl.BlockSpec((1, ts, HEAD_DIM), lambda b, s: (b, s, 0))
    vec = pl.BlockSpec((1, HEAD_DIM), lambda b, s: (0, 0))
    return pl.pallas_call(
        _rope_kernel,
        out_shape=(jax.ShapeDtypeStruct((B, S, HEAD_DIM), F32),) * 2,
        grid=(B, S // ts),
        in_specs=[blk, vec, vec],
        out_specs=(blk, blk),
        compiler_params=_params(("parallel", "parallel")),
        name="rope_tables",
    )(pos_b, inv_full, sign)


def _rope(x, cos, sin):
    return x * cos + pltpu.roll(x, HEAD_DIM // 2, axis=1) * sin


FFN_TM = 512
FFN_TF = 256


def _ffn_kernel(x_ref, wg_ref, wu_ref, wd_ref, g_ref, b_ref, o_ref, xb_ref):
    j = pl.program_id(1)

    @pl.when(j == 0)
    def _():
        xb_ref[...] = x_ref[...].astype(BF16)
        o_ref[...] = jnp.zeros_like(o_ref)

    xb = xb_ref[...]
    gate = jnp.dot(xb, wg_ref[...], preferred_element_type=F32)
    up = jnp.dot(xb, wu_ref[...], preferred_element_type=F32)
    act = (gate / (1.0 + jnp.exp(-gate)) * up).astype(BF16)
    _accumulate_dot(o_ref, act, wd_ref)

    @pl.when(j == pl.num_programs(1) - 1)
    def _():
        _residual_layer_norm(o_ref, x_ref, FFN_RES, g_ref, b_ref)


def ffn_sublayer(x, w_gu, w_down, ln_g, ln_b):
    T, D = x.shape
    tm = min(FFN_TM, T)
    nj = D_FF // FFN_TF
    vec = pl.BlockSpec((1, D), lambda i, j: (0, 0))
    return pl.pallas_call(
        _ffn_kernel,
        out_shape=jax.ShapeDtypeStruct((T, D), F32),
        grid=(T // tm, nj),
        in_specs=[
            pl.BlockSpec((tm, D), lambda i, j: (i, 0)),
            pl.BlockSpec((D, FFN_TF), lambda i, j: (0, j)),
            pl.BlockSpec((D, FFN_TF), lambda i, j: (0, j + nj)),
            pl.BlockSpec((FFN_TF, D), lambda i, j: (j, 0)),
            vec, vec,
        ],
        out_specs=pl.BlockSpec((tm, D), lambda i, j: (i, 0)),
        scratch_shapes=[pltpu.VMEM((tm, D), BF16)],
        compiler_params=_params(("parallel", "arbitrary")),
        name="ffn",
    )(x, w_gu, w_gu, w_down, ln_g, ln_b)


PROJ_TM = 512


def _proj_kernel(x_ref, w_ref, o_ref, xb_ref):
    @pl.when(pl.program_id(1) == 0)
    def _():
        xb_ref[...] = x_ref[...].astype(BF16)

    o_ref[...] = jnp.dot(xb_ref[...], w_ref[...], preferred_element_type=F32)


def in_projection(x, w_in_r):
    T, D = x.shape
    tm = min(PROJ_TM, T)
    return pl.pallas_call(
        _proj_kernel,
        out_shape=jax.ShapeDtypeStruct((T, NP), F32),
        grid=(T // tm, NP // PROJ_TN),
        in_specs=[
            pl.BlockSpec((tm, D), lambda i, j: (i, 0)),
            pl.BlockSpec((D, PROJ_TN), lambda i, j: (0, j)),
        ],
        out_specs=pl.BlockSpec((tm, PROJ_TN), lambda i, j: (i, j)),
        scratch_shapes=[pltpu.VMEM((tm, D), BF16)],
        compiler_params=_params(("parallel", "arbitrary")),
        name="in_proj",
    )(x, w_in_r)


OUT_TM = 256


def _outproj_kernel(x_ref, y0_ref, y1_ref, y2_ref, y3_ref, w_ref, g_ref, b_ref, o_ref):
    for c in range(0, o_ref.shape[1], ACC_COLS):
        cols = slice(c, c + ACC_COLS)
        acc = DN_ALPHA * x_ref[:, cols]
        for idx, y_ref in enumerate((y0_ref, y1_ref, y2_ref, y3_ref)):
            acc = acc + jnp.dot(y_ref[...], w_ref[idx * GROUP_W:(idx + 1) * GROUP_W, cols],
                                preferred_element_type=F32)
        o_ref[:, cols] = acc
    _residual_layer_norm(o_ref, None, 1.0, g_ref, b_ref)


def out_projection(x, ys, w_out, ln_g, ln_b):
    T, D = x.shape
    tm = min(OUT_TM, T)
    vec = pl.BlockSpec((1, D), lambda i: (0, 0))
    yspec = pl.BlockSpec((tm, GROUP_W), lambda i: (i, 0))
    return pl.pallas_call(
        _outproj_kernel,
        out_shape=jax.ShapeDtypeStruct((T, D), F32),
        grid=(T // tm,),
        in_specs=[
            pl.BlockSpec((tm, D), lambda i: (i, 0)),
            yspec, yspec, yspec, yspec,
            pl.BlockSpec(w_out.shape, lambda i: (0, 0), pipeline_mode=pl.Buffered(1)),
            vec, vec,
        ],
        out_specs=pl.BlockSpec((tm, D), lambda i: (i, 0)),
        compiler_params=_params(("parallel",)),
        name="out_proj",
    )(x, *ys, w_out, ln_g, ln_b)


GLA_ROWS = 512
SUBLANES = 8


def _cumsum_rows(x):
    n = x.shape[0]
    row = lax.broadcasted_iota(jnp.int32, x.shape, 0)
    sh = 1
    while sh < n:
        x = x + jnp.where(row >= sh, pltpu.roll(x, sh, axis=0), 0.0)
        sh *= 2
    return x


def _gla_kernel(q_ref, k_ref, v_ref, gg_ref, lr_ref, wg_ref, bg_ref, ng_ref, o_ref,
                st_ref, b_scr, k_scr, v_scr):
    C = GLA_CHUNK

    @pl.when(pl.program_id(2) == 0)
    def _():
        st_ref[...] = jnp.zeros_like(st_ref)

    n_chunks = q_ref.shape[1] // C
    row8 = lax.broadcasted_iota(jnp.int32, (SUBLANES, 1), 0)

    def chunk(c, carry):
        r0 = pl.multiple_of(c * C, C)
        rows = pl.ds(r0, C)
        q = q_ref[0, rows, :] * (GLA_DK ** -0.5)
        k = k_ref[0, rows, :]
        v = v_ref[0, rows, :]
        logits = jnp.dot(lr_ref[0, rows, :], wg_ref[...], preferred_element_type=F32,
                         precision=lax.Precision.HIGHEST) + bg_ref[...]
        g = (jnp.minimum(logits, 0.0) - jnp.log(1.0 + jnp.exp(-jnp.abs(logits)))) / GLA_GATE_NORMALIZER
        b = _cumsum_rows(g)
        b_last = b[C - 1:C, :]
        b_scr[...] = b
        k_scr[...] = k
        v_scr[...] = v

        st = st_ref[...]
        o_inter = lax.dot_general((q * jnp.exp(b)).astype(BF16), st.astype(BF16),
                                  (((1,), (1,)), ((), ())), preferred_element_type=F32)
        kd = (k * jnp.exp(b_last - b)).astype(BF16)
        st_ref[...] = st * jnp.exp(b_last) + lax.dot_general(
            v.astype(BF16), kd, (((0,), (0,)), ((), ())), preferred_element_type=F32)

        for ib in range(C // SUBLANES):
            i0 = ib * SUBLANES
            q_i = q[i0:i0 + SUBLANES]
            b_i = b[i0:i0 + SUBLANES]
            acc = o_inter[i0:i0 + SUBLANES]
            for j in range(i0 + SUBLANES):
                b_j = b_scr[j:j + 1, :]
                k_j = k_scr[j:j + 1, :]
                v_j = v_scr[j:j + 1, :]
                e = jnp.exp(jnp.minimum(b_i - b_j, 0.0))
                a = jnp.sum(q_i * k_j * e, axis=-1, keepdims=True)
                if j >= i0:
                    a = jnp.where(row8 >= (j - i0), a, 0.0)
                acc = acc + a * v_j
            acc = acc * lax.rsqrt(jnp.mean(acc * acc, axis=-1, keepdims=True) + RMS_EPS) * ng_ref[...]
            gg = gg_ref[0, pl.ds(r0 + i0, SUBLANES), :]
            o_ref[0, pl.ds(r0 + i0, SUBLANES), :] = (acc * (gg / (1.0 + jnp.exp(-gg)))).astype(o_ref.dtype)
        return carry

    lax.fori_loop(0, n_chunks, chunk, 0)


def gla_mixer(P, wg_pad, bg, ng):
    B, S, _ = P.shape
    R = min(GLA_ROWS, S)
    kb = lambda base: (lambda b, h, r: (b, r, base // GLA_DK + h))
    vb = lambda base: (lambda b, h, r: (b, r, base // GLA_DV + h))
    return pl.pallas_call(
        _gla_kernel,
        out_shape=jax.ShapeDtypeStruct((B, S, GROUP_W), BF16),
        grid=(B, GLA_HEADS, S // R),
        in_specs=[
            pl.BlockSpec((1, R, GLA_DK), kb(COL_GQ)),
            pl.BlockSpec((1, R, GLA_DK), kb(COL_GK)),
            pl.BlockSpec((1, R, GLA_DV), vb(COL_GV)),
            pl.BlockSpec((1, R, GLA_DV), vb(COL_GG)),
            pl.BlockSpec((1, R, LANES), lambda b, h, r: (b, r, COL_LR // LANES)),
            pl.BlockSpec((LANES, GLA_DK), lambda b, h, r: (0, h)),
            pl.BlockSpec((1, GLA_DK), lambda b, h, r: (0, h)),
            pl.BlockSpec((1, GLA_DV), lambda b, h, r: (0, 0)),
        ],
        out_specs=pl.BlockSpec((1, R, GLA_DV), lambda b, h, r: (b, r, h)),
        scratch_shapes=[
            pltpu.VMEM((GLA_DV, GLA_DK), F32),
            pltpu.VMEM((GLA_CHUNK, GLA_DK), F32),
            pltpu.VMEM((GLA_CHUNK, GLA_DK), F32),
            pltpu.VMEM((GLA_CHUNK, GLA_DV), F32),
        ],
        compiler_params=_params(("parallel", "parallel", "arbitrary")),
        name="gla",
    )(P, P, P, P, P, wg_pad, bg, ng)


def _swa_kernel(sink_ref, q_ref, kc_ref, kp_ref, vc_ref, vp_ref, cc_ref, sc_ref, cp_ref, sp_ref, o_ref):
    kv = pl.program_id(1)
    n = pl.program_id(2)
    BLK = SWA_BLOCK
    cos, sin = cc_ref[0], sc_ref[0]
    k_cat = jnp.concatenate([_rope(kp_ref[0], cp_ref[0], sp_ref[0]), _rope(kc_ref[0], cos, sin)],
                            axis=0).astype(BF16)
    v_cat = jnp.concatenate([vp_ref[0], vc_ref[0]], axis=0).astype(BF16)
    qpos = BLK + lax.broadcasted_iota(jnp.int32, (BLK, 2 * BLK), 0)
    kpos = lax.broadcasted_iota(jnp.int32, (BLK, 2 * BLK), 1)
    rel = qpos - kpos
    mask = (rel >= 0) & (rel < SWA_WINDOW) & ((kpos >= BLK) | (n > 0))
    for g in range(SWA_GROUP):
        cols = slice(g * HEAD_DIM, (g + 1) * HEAD_DIM)
        q = _rope(q_ref[0, :, cols], cos, sin).astype(BF16)
        s = lax.dot_general(q, k_cat, (((1,), (1,)), ((), ())), preferred_element_type=F32) * (HEAD_DIM ** -0.5)
        s = jnp.where(mask, s, NEG_INF)
        sink = sink_ref[kv * SWA_GROUP + g]
        m = jnp.maximum(jnp.max(s, axis=-1, keepdims=True), sink)
        p = jnp.exp(s - m)
        den = jnp.sum(p, axis=-1, keepdims=True) + jnp.exp(sink - m)
        o = jnp.dot(p.astype(BF16), v_cat, preferred_element_type=F32) / den
        o_ref[0, :, cols] = o.astype(o_ref.dtype)


def swa_mixer(P, cos, sin, sinks):
    B, S, _ = P.shape
    BLK = SWA_BLOCK
    cur = lambda base: (lambda b, kv, n, s: (b, n, base // HEAD_DIM + kv))
    prev = lambda base: (lambda b, kv, n, s: (b, jnp.maximum(n - 1, 0), base // HEAD_DIM + kv))
    hd = lambda f: pl.BlockSpec((1, BLK, HEAD_DIM), f)
    qw = SWA_GROUP * HEAD_DIM
    return pl.pallas_call(
        _swa_kernel,
        out_shape=jax.ShapeDtypeStruct((B, S, GROUP_W), BF16),
        grid_spec=pltpu.PrefetchScalarGridSpec(
            num_scalar_prefetch=1,
            grid=(B, SWA_KV_HEADS, S // BLK),
            in_specs=[
                pl.BlockSpec((1, BLK, qw), lambda b, kv, n, s: (b, n, COL_SQ // qw + kv)),
                hd(cur(COL_SK)), hd(prev(COL_SK)), hd(cur(COL_SV)), hd(prev(COL_SV)),
                hd(lambda b, kv, n, s: (b, n, 0)), hd(lambda b, kv, n, s: (b, n, 0)),
                hd(lambda b, kv, n, s: (b, jnp.maximum(n - 1, 0), 0)),
                hd(lambda b, kv, n, s: (b, jnp.maximum(n - 1, 0), 0)),
            ],
            out_specs=pl.BlockSpec((1, BLK, qw), lambda b, kv, n, s: (b, n, kv)),
        ),
        compiler_params=_params(("parallel", "parallel", "arbitrary")),
        name="swa",
    )(sinks, P, P, P, P, P, cos, sin, cos, sin)


def _moba_kernel(q_ref, k_ref, v_ref, cos_ref, sin_ref, o_ref, kr_ref, vb_ref, kbar_ref):
    i = pl.program_id(2)
    BLK = MOBA_BLOCK
    S = k_ref.shape[1]
    NB = S // BLK

    @pl.when(i == 0)
    def _():
        kr = _rope(k_ref[0], cos_ref[0], sin_ref[0])
        kr_ref[...] = kr.astype(BF16)
        vb_ref[...] = v_ref[0].astype(BF16)
        for n in range(NB):
            kbar_ref[n:n + 1, :] = jnp.mean(kr[n * BLK:(n + 1) * BLK], axis=0, keepdims=True)

    rows = pl.ds(pl.multiple_of(i * BLK, BLK), BLK)
    q = _rope(q_ref[0], cos_ref[0, rows, :], sin_ref[0, rows, :])
    nt = (((1,), (1,)), ((), ()))

    gate = lax.dot_general(kbar_ref[...], q, nt, preferred_element_type=F32,
                           precision=lax.Precision.HIGHEST)
    blk = lax.broadcasted_iota(jnp.int32, (NB, BLK), 0)
    past = blk < i
    sel = []
    for n in range(NB):
        g_n = gate[n:n + 1, :]
        ahead = ((gate > g_n) | ((gate == g_n) & (blk < n))) & past
        rank = jnp.sum(jnp.where(ahead, 1.0, 0.0), axis=0, keepdims=True)
        sel.append(rank < float(MOBA_TOPK))

    qb = q.astype(BF16)
    key = lax.broadcasted_iota(jnp.int32, (BLK, BLK), 0)
    qry = lax.broadcasted_iota(jnp.int32, (BLK, BLK), 1)
    causal = key <= qry

    def attend(nb):
        scores = [lax.dot_general(kr_ref[n * BLK:(n + 1) * BLK, :], qb, nt,
                                  preferred_element_type=F32) for n in range(nb)]
        masked = [jnp.where((sel[n] & (n < i)) | (causal & (n == i)), scores[n] * (HEAD_DIM ** -0.5), NEG_INF)
                  for n in range(nb)]
        m = functools.reduce(jnp.maximum, [jnp.max(s, axis=0, keepdims=True) for s in masked])
        probs = [jnp.exp(s - m) for s in masked]
        den = functools.reduce(jnp.add, [jnp.sum(p, axis=0, keepdims=True) for p in probs])
        outs = [lax.dot_general(probs[n].astype(BF16), vb_ref[n * BLK:(n + 1) * BLK, :],
                                (((0,), (0,)), ((), ())), preferred_element_type=F32) for n in range(nb)]
        inv = jnp.transpose(jnp.broadcast_to(1.0 / den, (HEAD_DIM, BLK)))
        o_ref[0] = (functools.reduce(jnp.add, outs) * inv).astype(o_ref.dtype)

    for nb in range(2, NB + 1, 2):
        @pl.when((i >= nb - 2) & (i < nb))
        def _(nb=nb):
            attend(nb)


def moba_mixer(P, cos, sin):
    B, S, _ = P.shape
    BLK = MOBA_BLOCK
    full = lambda base: pl.BlockSpec((1, S, HEAD_DIM), lambda b, h, i: (b, 0, base // HEAD_DIM + h))
    tab = pl.BlockSpec((1, S, HEAD_DIM), lambda b, h, i: (b, 0, 0))
    return pl.pallas_call(
        _moba_kernel,
        out_shape=jax.ShapeDtypeStruct((B, S, GROUP_W), BF16),
        grid=(B, MOBA_HEADS, S // BLK),
        in_specs=[
            pl.BlockSpec((1, BLK, HEAD_DIM), lambda b, h, i: (b, i, COL_MQ // HEAD_DIM + h)),
            full(COL_MK), full(COL_MV), tab, tab,
        ],
        out_specs=pl.BlockSpec((1, BLK, HEAD_DIM), lambda b, h, i: (b, i, h)),
        scratch_shapes=[
            pltpu.VMEM((S, HEAD_DIM), BF16),
            pltpu.VMEM((S, HEAD_DIM), BF16),
            pltpu.VMEM((S // BLK, HEAD_DIM), F32),
        ],
        compiler_params=_params(("parallel", "parallel", "arbitrary")),
        name="moba",
    )(P, P, P, cos, sin)


SB_TILE = 256


SB_GROUP = 4


def _sb_kernel(q_ref, k_ref, v_ref, o_ref, qb_ref, run_ref, acc_ref):
    i = pl.program_id(2)
    T = SB_TILE
    qb_ref[...] = q_ref[0].astype(BF16)
    srow = lax.broadcasted_iota(jnp.int32, (T, T), 0)
    scol = lax.broadcasted_iota(jnp.int32, (T, T), 1)
    upper = jnp.where(srow > scol, 1.0, 0.0).astype(BF16)
    strict = scol < srow

    def tile(j, diagonal):
        rows = pl.ds(pl.multiple_of(j * T, T), T)
        heads = range(SB_GROUP)
        cols = [slice(h * HEAD_DIM, (h + 1) * HEAD_DIM) for h in heads]
        z = [lax.dot_general(qb_ref[:, cols[h]], k_ref[0, rows, cols[h]].astype(BF16), (((1,), (1,)), ((), ())),
                             preferred_element_type=F32) * (HEAD_DIM ** -0.5) for h in heads]
        soft = [jnp.log(1.0 + jnp.exp(-jnp.abs(z[h]))) for h in heads]
        log_beta = [jnp.minimum(z[h], 0.0) - soft[h] for h in heads]
        log_keep = [jnp.minimum(-z[h], 0.0) - soft[h] for h in heads]
        if diagonal:
            log_keep = [jnp.where(strict, log_keep[h], 0.0) for h in heads]
        hi = [log_keep[h].astype(BF16) for h in heads]
        lo = [(log_keep[h] - hi[h].astype(F32)).astype(BF16) for h in heads]
        suffix = [jnp.dot(hi[h], upper, preferred_element_type=F32)
                  + jnp.dot(lo[h], upper, preferred_element_type=F32) for h in heads]
        total = [jnp.sum(log_keep[h], axis=-1, keepdims=True) for h in heads]
        if diagonal:
            w = [jnp.where(strict, jnp.exp(log_beta[h] + suffix[h]), 0.0) for h in heads]
        else:
            w = [jnp.exp(log_beta[h] + (suffix[h] + run_ref[h])) for h in heads]
        pv = [jnp.dot(w[h].astype(BF16), v_ref[0, rows, cols[h]].astype(BF16), preferred_element_type=F32)
              for h in heads]
        for h in heads:
            if diagonal:
                acc_ref[h] = pv[h]
                run_ref[h] = total[h]
            else:
                acc_ref[h] += pv[h]
                run_ref[h] += total[h]

    tile(i, True)

    def body(t, carry):
        tile(i - 1 - t, False)
        return carry

    lax.fori_loop(0, i, body, 0)
    for h in range(SB_GROUP):
        o_ref[0, :, h * HEAD_DIM:(h + 1) * HEAD_DIM] = acc_ref[h].astype(o_ref.dtype)


def sb_mixer(P):
    B, S, _ = P.shape
    T = SB_TILE
    W = SB_GROUP * HEAD_DIM
    full = lambda base: pl.BlockSpec((1, S, W), lambda b, h, i: (b, 0, base // W + h))
    return pl.pallas_call(
        _sb_kernel,
        out_shape=jax.ShapeDtypeStruct((B, S, GROUP_W), BF16),
        grid=(B, SB_HEADS // SB_GROUP, S // T),
        in_specs=[
            pl.BlockSpec((1, T, W), lambda b, h, i: (b, i, COL_BQ // W + h)),
            full(COL_BK), full(COL_BV),
        ],
        out_specs=pl.BlockSpec((1, T, W), lambda b, h, i: (b, i, h)),
        scratch_shapes=[
            pltpu.VMEM((T, W), BF16),
            pltpu.VMEM((SB_GROUP, T, 1), F32),
            pltpu.VMEM((SB_GROUP, T, HEAD_DIM), F32),
        ],
        compiler_params=_params(("parallel", "parallel", "arbitrary")),
        name="stick_breaking",
    )(P, P, P)


CAST_ROWS = 256
CAST_BLOCK_BYTES = 6 * 1024 * 1024


def _cast_kernel(w_ref, o_ref):
    o_ref[...] = w_ref[...].astype(o_ref.dtype)


def cast_bf16(w, layer):
    _, R, C = w.shape
    tr = min(CAST_ROWS, R)
    tc = next(C // k for k in range(1, C // LANES + 1)
              if C % k == 0 and (C // k) % LANES == 0 and tr * (C // k) * 4 <= CAST_BLOCK_BYTES)
    return pl.pallas_call(
        _cast_kernel,
        out_shape=jax.ShapeDtypeStruct((R, C), BF16),
        grid=(R // tr, C // tc),
        in_specs=[pl.BlockSpec((None, tr, tc), lambda i, j: (layer, i, j))],
        out_specs=pl.BlockSpec((tr, tc), lambda i, j: (i, j)),
        compiler_params=_params(("parallel", "parallel")),
        name="cast_bf16",
    )(w)


def _relayout_w_in(w):
    lr0 = COL_GG + GROUP_W
    lr1 = lr0 + GLA_GATE_RANK
    wb = w.astype(BF16)
    pad = jnp.zeros((w.shape[0], NP - w.shape[1]), BF16)
    return jnp.concatenate([wb[:, :lr0], wb[:, lr1:], wb[:, lr0:lr1], pad], axis=1)


def token_mixing(x2, B, S, cos, sin, w_in_r, wg_pad, bg, ng, sinks, w_out_b, ln_g, ln_b):
    P = in_projection(x2, w_in_r).reshape(B, S, NP)
    ys = (gla_mixer(P, wg_pad, bg, ng), swa_mixer(P, cos, sin, sinks), moba_mixer(P, cos, sin), sb_mixer(P))
    ys = [y.reshape(B * S, GROUP_W) for y in ys]
    return out_projection(x2, ys, w_out_b, ln_g, ln_b)


def kernel(x, positions, w_in, gla_w_gate_up, gla_b_gate_up, gla_norm_g, swa_sinks, w_out,
           ffn1_w_gu, ffn1_w_down, ffn2_w_gu, ffn2_w_down, ln_g, ln_b):
    B, S, D = x.shape
    cos, sin = rope_tables(positions)
    x2 = x.reshape(B * S, D)
    for l in range(DEPTH):
        g = ln_g[l].reshape(3, 1, D)
        b = ln_b[l].reshape(3, 1, D)
        x2 = ffn_sublayer(x2, cast_bf16(ffn1_w_gu, l), cast_bf16(ffn1_w_down, l), g[0], b[0])
        wg_pad = jnp.zeros((LANES, GLA_KEY), F32).at[:GLA_GATE_RANK].set(gla_w_gate_up[l])
        x2 = token_mixing(x2, B, S, cos, sin, _relayout_w_in(w_in[l]), wg_pad,
                          gla_b_gate_up[l].reshape(1, GLA_KEY), gla_norm_g[l].reshape(1, GLA_DV),
                          swa_sinks[l], cast_bf16(w_out, l), g[1], b[1])
        x2 = ffn_sublayer(x2, cast_bf16(ffn2_w_gu, l), cast_bf16(ffn2_w_down, l), g[2], b[2])
    return x2.reshape(B, S, D)
```

```python
import functools

import jax
import jax.numpy as jnp
from jax import lax
from jax.experimental import pallas as pl
from jax.experimental.pallas import tpu as pltpu

F32 = jnp.float32
BF16 = jnp.bfloat16

D_MODEL = 4096
DEPTH = 2
GROUP_W = 1024
HEAD_DIM = 128
GLA_HEADS = 4
GLA_DV = 256
GLA_DK = 128
GLA_KEY = 512
GLA_GATE_RANK = 16
GLA_GATE_NORMALIZER = 16.0
GLA_CHUNK = 64
SWA_HEADS = 8
SWA_KV_HEADS = 2
SWA_GROUP = SWA_HEADS // SWA_KV_HEADS
SWA_WINDOW = 128
SWA_BLOCK = 128
MOBA_HEADS = 8
MOBA_BLOCK = 256
MOBA_TOPK = 3
SB_HEADS = 8
ROPE_THETA = 10000.0
D_FF = 11008
FFN_RES = 0.5
LN_EPS = 1e-5
RMS_EPS = 1e-5
DN_ALPHA = (2 * DEPTH) ** 0.25

LANES = 128
VMEM_LIMIT = 56 * 1024 * 1024

COL_GQ, COL_GK, COL_GV, COL_GG = 0, 512, 1024, 2048
COL_SQ, COL_SK, COL_SV = 3072, 4096, 4352
COL_MQ, COL_MK, COL_MV = 4608, 5632, 6656
COL_BQ, COL_BK, COL_BV = 7680, 8704, 9728
COL_LR = 10752
PROJ_TN = 1024
NP = 11264

NEG_INF = float("-inf")


def _params(sem, vmem=VMEM_LIMIT):
    return pltpu.CompilerParams(dimension_semantics=sem, vmem_limit_bytes=vmem)


def _layer_norm_rows(y, g, b):
    mu = jnp.mean(y, axis=-1, keepdims=True)
    yc = y - mu
    var = jnp.mean(yc * yc, axis=-1, keepdims=True)
    return yc * lax.rsqrt(var + LN_EPS) * g + b


LN_ROWS = 32
ACC_COLS = 512


def _residual_layer_norm(o_ref, x_ref, res_scale, g_ref, b_ref):
    def body(r, carry):
        rows = pl.ds(pl.multiple_of(r * LN_ROWS, LN_ROWS), LN_ROWS)
        y = o_ref[rows, :]
        if x_ref is not None:
            y = DN_ALPHA * x_ref[rows, :] + res_scale * y
        o_ref[rows, :] = _layer_norm_rows(y, g_ref[...], b_ref[...])
        return carry

    lax.fori_loop(0, o_ref.shape[0] // LN_ROWS, body, 0)


def _accumulate_dot(o_ref, a, w_ref):
    for c in range(0, o_ref.shape[1], ACC_COLS):
        o_ref[:, c:c + ACC_COLS] += jnp.dot(a, w_ref[:, c:c + ACC_COLS], preferred_element_type=F32)


def _rope_kernel(pos_ref, inv_ref, sign_ref, cos_ref, sin_ref):
    ang = pos_ref[0].astype(F32) * inv_ref[...]
    cos_ref[0] = jnp.cos(ang)
    sin_ref[0] = jnp.sin(ang) * sign_ref[...]


def rope_tables(positions):
    B, S = positions.shape
    ts = min(S, 512)
    inv = 1.0 / (ROPE_THETA ** (jnp.arange(0, HEAD_DIM, 2, dtype=F32) / HEAD_DIM))
    inv_full = jnp.concatenate([inv, inv]).reshape(1, HEAD_DIM)
    sign = jnp.concatenate([-jnp.ones((HEAD_DIM // 2,), F32), jnp.ones((HEAD_DIM // 2,), F32)]).reshape(1, HEAD_DIM)
    pos_b = jnp.broadcast_to(positions[:, :, None], (B, S, HEAD_DIM))
    blk = pl.BlockSpec((1, ts, HEAD_DIM), lambda b, s: (b, s, 0))
    vec = pl.BlockSpec((1, HEAD_DIM), lambda b, s: (0, 0))
    return pl.pallas_call(
        _rope_kernel,
        out_shape=(jax.ShapeDtypeStruct((B, S, HEAD_DIM), F32),) * 2,
        grid=(B, S // ts),
        in_specs=[blk, vec, vec],
        out_specs=(blk, blk),
        compiler_params=_params(("parallel", "parallel")),
        name="rope_tables",
    )(pos_b, inv_full, sign)


def _rope(x, cos, sin):
    return x * cos + pltpu.roll(x, HEAD_DIM // 2, axis=1) * sin


FFN_TM = 512
FFN_TF = 256


def _ffn_kernel(x_ref, wg_ref, wu_ref, wd_ref, g_ref, b_ref, o_ref, xb_ref):
    j = pl.program_id(1)

    @pl.when(j == 0)
    def _():
        xb_ref[...] = x_ref[...].astype(BF16)
        o_ref[...] = jnp.zeros_like(o_ref)

    xb = xb_ref[...]
    gate = jnp.dot(xb, wg_ref[...], preferred_element_type=F32)
    up = jnp.dot(xb, wu_ref[...], preferred_element_type=F32)
    act = (gate / (1.0 + jnp.exp(-gate)) * up).astype(BF16)
    _accumulate_dot(o_ref, act, wd_ref)

    @pl.when(j == pl.num_programs(1) - 1)
    def _():
        _residual_layer_norm(o_ref, x_ref, FFN_RES, g_ref, b_ref)


def ffn_sublayer(x, w_gu, w_down, ln_g, ln_b):
    T, D = x.shape
    tm = min(FFN_TM, T)
    nj = D_FF // FFN_TF
    vec = pl.BlockSpec((1, D), lambda i, j: (0, 0))
    return pl.pallas_call(
        _ffn_kernel,
        out_shape=jax.ShapeDtypeStruct((T, D), F32),
        grid=(T // tm, nj),
        in_specs=[
            pl.BlockSpec((tm, D), lambda i, j: (i, 0)),
            pl.BlockSpec((D, FFN_TF), lambda i, j: (0, j)),
            pl.BlockSpec((D, FFN_TF), lambda i, j: (0, j + nj)),
            pl.BlockSpec((FFN_TF, D), lambda i, j: (j, 0)),
            vec, vec,
        ],
        out_specs=pl.BlockSpec((tm, D), lambda i, j: (i, 0)),
        scratch_shapes=[pltpu.VMEM((tm, D), BF16)],
        compiler_params=_params(("parallel", "arbitrary")),
        name="ffn",
    )(x, w_gu, w_gu, w_down, ln_g, ln_b)


PROJ_TM = 512


def _proj_kernel(x_ref, w_ref, o_ref, xb_ref):
    @pl.when(pl.program_id(1) == 0)
    def _():
        xb_ref[...] = x_ref[...].astype(BF16)

    o_ref[...] = jnp.dot(xb_ref[...], w_ref[...], preferred_element_type=F32)


def in_projection(x, w_in_r):
    T, D = x.shape
    tm = min(PROJ_TM, T)
    return pl.pallas_call(
        _proj_kernel,
        out_shape=jax.ShapeDtypeStruct((T, NP), F32),
        grid=(T // tm, NP // PROJ_TN),
        in_specs=[
            pl.BlockSpec((tm, D), lambda i, j: (i, 0)),
            pl.BlockSpec((D, PROJ_TN), lambda i, j: (0, j)),
        ],
        out_specs=pl.BlockSpec((tm, PROJ_TN), lambda i, j: (i, j)),
        scratch_shapes=[pltpu.VMEM((tm, D), BF16)],
        compiler_params=_params(("parallel", "arbitrary")),
        name="in_proj",
    )(x, w_in_r)


OUT_TM = 256


def _outproj_kernel(x_ref, y0_ref, y1_ref, y2_ref, y3_ref, w_ref, g_ref, b_ref, o_ref):
    for c in range(0, o_ref.shape[1], ACC_COLS):
        cols = slice(c, c + ACC_COLS)
        acc = DN_ALPHA * x_ref[:, cols]
        for idx, y_ref in enumerate((y0_ref, y1_ref, y2_ref, y3_ref)):
            acc = acc + jnp.dot(y_ref[...], w_ref[idx * GROUP_W:(idx + 1) * GROUP_W, cols],
                                preferred_element_type=F32)
        o_ref[:, cols] = acc
    _residual_layer_norm(o_ref, None, 1.0, g_ref, b_ref)


def out_projection(x, ys, w_out, ln_g, ln_b):
    T, D = x.shape
    tm = min(OUT_TM, T)
    vec = pl.BlockSpec((1, D), lambda i: (0, 0))
    yspec = pl.BlockSpec((tm, GROUP_W), lambda i: (i, 0))
    return pl.pallas_call(
        _outproj_kernel,
        out_shape=jax.ShapeDtypeStruct((T, D), F32),
        grid=(T // tm,),
        in_specs=[
            pl.BlockSpec((tm, D), lambda i: (i, 0)),
            yspec, yspec, yspec, yspec,
            pl.BlockSpec(w_out.shape, lambda i: (0, 0), pipeline_mode=pl.Buffered(1)),
            vec, vec,
        ],
        out_specs=pl.BlockSpec((tm, D), lambda i: (i, 0)),
        compiler_params=_params(("parallel",)),
        name="out_proj",
    )(x, *ys, w_out, ln_g, ln_b)


GLA_ROWS = 512
SUBLANES = 8
GLA_SUB = 16
GLA_GROUP = 4


def _cumsum_rows(x):
    n = x.shape[0]
    row = lax.broadcasted_iota(jnp.int32, x.shape, 0)
    sh = 1
    while sh < n:
        x = x + jnp.where(row >= sh, pltpu.roll(x, sh, axis=0), 0.0)
        sh *= 2
    return x


def _gla_kernel(q_ref, k_ref, v_ref, gg_ref, lr_ref, wg_ref, bg_ref, ng_ref, o_ref,
                st_ref, b_scr, k_scr, v_scr):
    C = GLA_CHUNK
    heads = range(GLA_GROUP)
    kcols = [slice(h * GLA_DK, (h + 1) * GLA_DK) for h in heads]
    vcols = [slice(h * GLA_DV, (h + 1) * GLA_DV) for h in heads]
    nt = (((1,), (1,)), ((), ()))

    @pl.when(pl.program_id(2) == 0)
    def _():
        st_ref[...] = jnp.zeros_like(st_ref)

    n_chunks = q_ref.shape[1] // C
    row8 = lax.broadcasted_iota(jnp.int32, (SUBLANES, 1), 0)

    def chunk(c, carry):
        r0 = pl.multiple_of(c * C, C)
        rows = pl.ds(r0, C)
        lr = lr_ref[0, rows, :]
        logits = [jnp.dot(lr, wg_ref[:, kcols[h]], preferred_element_type=F32,
                          precision=lax.Precision.HIGHEST) + bg_ref[:, kcols[h]] for h in heads]
        q = [q_ref[0, rows, kcols[h]] * (GLA_DK ** -0.5) for h in heads]
        k = [k_ref[0, rows, kcols[h]] for h in heads]
        v = [v_ref[0, rows, vcols[h]] for h in heads]
        g = [(jnp.minimum(x, 0.0) - jnp.log(1.0 + jnp.exp(-jnp.abs(x)))) / GLA_GATE_NORMALIZER for x in logits]
        b = [_cumsum_rows(x) for x in g]
        b_last = [x[C - 1:C, :] for x in b]
        for h in heads:
            b_scr[h] = b[h]
            k_scr[h] = k[h]
            v_scr[h] = v[h]
        st = [st_ref[h] for h in heads]
        o_inter = [lax.dot_general((q[h] * jnp.exp(b[h])).astype(BF16), st[h].astype(BF16), nt,
                                   preferred_element_type=F32) for h in heads]
        v_b = [x.astype(BF16) for x in v]
        for h in heads:
            kd = (k[h] * jnp.exp(b_last[h] - b[h])).astype(BF16)
            st_ref[h] = st[h] * jnp.exp(b_last[h]) + lax.dot_general(
                v_b[h], kd, (((0,), (0,)), ((), ())), preferred_element_type=F32)

        for s in range(0, C, GLA_SUB):
            blk_acc = [o_inter[h][s:s + GLA_SUB] for h in heads]
            if s > 0:
                att = []
                for h in heads:
                    b_s = b_scr[h, s:s + 1, :]
                    q_t = (q[h][s:s + GLA_SUB] * jnp.exp(b[h][s:s + GLA_SUB] - b_s)).astype(BF16)
                    k_t = jnp.concatenate([k[h][:s] * jnp.exp(b_s - b[h][:s]), jnp.zeros((C - s, GLA_DK), F32)],
                                          axis=0).astype(BF16)
                    att.append(lax.dot_general(q_t, k_t, nt, preferred_element_type=F32))
                blk_acc = [blk_acc[h] + jnp.dot(att[h].astype(BF16), v_b[h], preferred_element_type=F32)
                           for h in heads]
            for i0 in range(s, s + GLA_SUB, SUBLANES):
                for h in heads:
                    q_i = q[h][i0:i0 + SUBLANES]
                    b_i = b[h][i0:i0 + SUBLANES]
                    acc = blk_acc[h][i0 - s:i0 - s + SUBLANES]
                    for j in range(s, i0 + SUBLANES):
                        b_j = b_scr[h, j:j + 1, :]
                        k_j = k_scr[h, j:j + 1, :]
                        v_j = v_scr[h, j:j + 1, :]
                        e = jnp.exp(jnp.minimum(b_i - b_j, 0.0))
                        a = jnp.sum(q_i * k_j * e, axis=-1, keepdims=True)
                        if j >= i0:
                            a = jnp.where(row8 >= (j - i0), a, 0.0)
                        acc = acc + a * v_j
                    acc = acc * lax.rsqrt(jnp.mean(acc * acc, axis=-1, keepdims=True) + RMS_EPS) * ng_ref[...]
                    gg = gg_ref[0, pl.ds(r0 + i0, SUBLANES), vcols[h]]
                    o_ref[0, pl.ds(r0 + i0, SUBLANES), vcols[h]] = (
                        acc * (gg / (1.0 + jnp.exp(-gg)))).astype(o_ref.dtype)
        return carry

    lax.fori_loop(0, n_chunks, chunk, 0)


def gla_mixer(P, wg_pad, bg, ng):
    B, S, _ = P.shape
    R = min(GLA_ROWS, S)
    kw = GLA_GROUP * GLA_DK
    vw = GLA_GROUP * GLA_DV
    kb = lambda base: (lambda b, h, r: (b, r, base // kw + h))
    vb = lambda base: (lambda b, h, r: (b, r, base // vw + h))
    return pl.pallas_call(
        _gla_kernel,
        out_shape=jax.ShapeDtypeStruct((B, S, GROUP_W), BF16),
        grid=(B, GLA_HEADS // GLA_GROUP, S // R),
        in_specs=[
            pl.BlockSpec((1, R, kw), kb(COL_GQ)),
            pl.BlockSpec((1, R, kw), kb(COL_GK)),
            pl.BlockSpec((1, R, vw), vb(COL_GV)),
            pl.BlockSpec((1, R, vw), vb(COL_GG)),
            pl.BlockSpec((1, R, LANES), lambda b, h, r: (b, r, COL_LR // LANES)),
            pl.BlockSpec((LANES, kw), lambda b, h, r: (0, h)),
            pl.BlockSpec((1, kw), lambda b, h, r: (0, h)),
            pl.BlockSpec((1, GLA_DV), lambda b, h, r: (0, 0)),
        ],
        out_specs=pl.BlockSpec((1, R, vw), lambda b, h, r: (b, r, h)),
        scratch_shapes=[
            pltpu.VMEM((GLA_GROUP, GLA_DV, GLA_DK), F32),
            pltpu.VMEM((GLA_GROUP, GLA_CHUNK, GLA_DK), F32),
            pltpu.VMEM((GLA_GROUP, GLA_CHUNK, GLA_DK), F32),
            pltpu.VMEM((GLA_GROUP, GLA_CHUNK, GLA_DV), F32),
        ],
        compiler_params=_params(("parallel", "parallel", "arbitrary")),
        name="gla",
    )(P, P, P, P, P, wg_pad, bg, ng)


def _swa_kernel(sink_ref, q_ref, kc_ref, kp_ref, vc_ref, vp_ref, cc_ref, sc_ref, cp_ref, sp_ref, o_ref):
    kv = pl.program_id(1)
    n = pl.program_id(2)
    BLK = SWA_BLOCK
    cos, sin = cc_ref[0], sc_ref[0]
    k_cat = jnp.concatenate([_rope(kp_ref[0], cp_ref[0], sp_ref[0]), _rope(kc_ref[0], cos, sin)],
                            axis=0).astype(BF16)
    v_cat = jnp.concatenate([vp_ref[0], vc_ref[0]], axis=0).astype(BF16)
    qpos = BLK + lax.broadcasted_iota(jnp.int32, (BLK, 2 * BLK), 0)
    kpos = lax.broadcasted_iota(jnp.int32, (BLK, 2 * BLK), 1)
    rel = qpos - kpos
    mask = (rel >= 0) & (rel < SWA_WINDOW) & ((kpos >= BLK) | (n > 0))
    for g in range(SWA_GROUP):
        cols = slice(g * HEAD_DIM, (g + 1) * HEAD_DIM)
        q = _rope(q_ref[0, :, cols], cos, sin).astype(BF16)
        s = lax.dot_general(q, k_cat, (((1,), (1,)), ((), ())), preferred_element_type=F32) * (HEAD_DIM ** -0.5)
        s = jnp.where(mask, s, NEG_INF)
        sink = sink_ref[kv * SWA_GROUP + g]
        m = jnp.maximum(jnp.max(s, axis=-1, keepdims=True), sink)
        p = jnp.exp(s - m)
        den = jnp.sum(p, axis=-1, keepdims=True) + jnp.exp(sink - m)
        o = jnp.dot(p.astype(BF16), v_cat, preferred_element_type=F32) / den
        o_ref[0, :, cols] = o.astype(o_ref.dtype)


def swa_mixer(P, cos, sin, sinks):
    B, S, _ = P.shape
    BLK = SWA_BLOCK
    cur = lambda base: (lambda b, kv, n, s: (b, n, base // HEAD_DIM + kv))
    prev = lambda base: (lambda b, kv, n, s: (b, jnp.maximum(n - 1, 0), base // HEAD_DIM + kv))
    hd = lambda f: pl.BlockSpec((1, BLK, HEAD_DIM), f)
    qw = SWA_GROUP * HEAD_DIM
    return pl.pallas_call(
        _swa_kernel,
        out_shape=jax.ShapeDtypeStruct((B, S, GROUP_W), BF16),
        grid_spec=pltpu.PrefetchScalarGridSpec(
            num_scalar_prefetch=1,
            grid=(B, SWA_KV_HEADS, S // BLK),
            in_specs=[
                pl.BlockSpec((1, BLK, qw), lambda b, kv, n, s: (b, n, COL_SQ // qw + kv)),
                hd(cur(COL_SK)), hd(prev(COL_SK)), hd(cur(COL_SV)), hd(prev(COL_SV)),
                hd(lambda b, kv, n, s: (b, n, 0)), hd(lambda b, kv, n, s: (b, n, 0)),
                hd(lambda b, kv, n, s: (b, jnp.maximum(n - 1, 0), 0)),
                hd(lambda b, kv, n, s: (b, jnp.maximum(n - 1, 0), 0)),
            ],
            out_specs=pl.BlockSpec((1, BLK, qw), lambda b, kv, n, s: (b, n, kv)),
        ),
        compiler_params=_params(("parallel", "parallel", "arbitrary")),
        name="swa",
    )(sinks, P, P, P, P, P, cos, sin, cos, sin)


def _moba_kernel(q_ref, k_ref, v_ref, cos_ref, sin_ref, o_ref, kr_ref, vb_ref, kbar_ref):
    i = pl.program_id(2)
    BLK = MOBA_BLOCK
    S = k_ref.shape[1]
    NB = S // BLK

    @pl.when(i == 0)
    def _():
        kr = _rope(k_ref[0], cos_ref[0], sin_ref[0])
        kr_ref[...] = kr.astype(BF16)
        vb_ref[...] = v_ref[0].astype(BF16)
        for n in range(NB):
            kbar_ref[n:n + 1, :] = jnp.mean(kr[n * BLK:(n + 1) * BLK], axis=0, keepdims=True)

    rows = pl.ds(pl.multiple_of(i * BLK, BLK), BLK)
    q = _rope(q_ref[0], cos_ref[0, rows, :], sin_ref[0, rows, :])
    nt = (((1,), (1,)), ((), ()))

    gate = lax.dot_general(kbar_ref[...], q, nt, preferred_element_type=F32,
                           precision=lax.Precision.HIGHEST)
    blk = lax.broadcasted_iota(jnp.int32, (NB, BLK), 0)
    past = blk < i
    sel = []
    for n in range(NB):
        g_n = gate[n:n + 1, :]
        ahead = ((gate > g_n) | ((gate == g_n) & (blk < n))) & past
        rank = jnp.sum(jnp.where(ahead, 1.0, 0.0), axis=0, keepdims=True)
        sel.append(rank < float(MOBA_TOPK))

    qb = q.astype(BF16)
    key = lax.broadcasted_iota(jnp.int32, (BLK, BLK), 0)
    qry = lax.broadcasted_iota(jnp.int32, (BLK, BLK), 1)
    causal = key <= qry

    def attend(nb):
        scores = [lax.dot_general(kr_ref[n * BLK:(n + 1) * BLK, :], qb, nt,
                                  preferred_element_type=F32) for n in range(nb)]
        masked = [jnp.where((sel[n] & (n < i)) | (causal & (n == i)), scores[n] * (HEAD_DIM ** -0.5), NEG_INF)
                  for n in range(nb)]
        m = functools.reduce(jnp.maximum, [jnp.max(s, axis=0, keepdims=True) for s in masked])
        probs = [jnp.exp(s - m) for s in masked]
        den = functools.reduce(jnp.add, [jnp.sum(p, axis=0, keepdims=True) for p in probs])
        outs = [lax.dot_general(probs[n].astype(BF16), vb_ref[n * BLK:(n + 1) * BLK, :],
                                (((0,), (0,)), ((), ())), preferred_element_type=F32) for n in range(nb)]
        inv = jnp.transpose(jnp.broadcast_to(1.0 / den, (HEAD_DIM, BLK)))
        o_ref[0] = (functools.reduce(jnp.add, outs) * inv).astype(o_ref.dtype)

    for nb in range(2, NB + 1, 2):
        @pl.when((i >= nb - 2) & (i < nb))
        def _(nb=nb):
            attend(nb)


def moba_mixer(P, cos, sin):
    B, S, _ = P.shape
    BLK = MOBA_BLOCK
    full = lambda base: pl.BlockSpec((1, S, HEAD_DIM), lambda b, h, i: (b, 0, base // HEAD_DIM + h))
    tab = pl.BlockSpec((1, S, HEAD_DIM), lambda b, h, i: (b, 0, 0))
    return pl.pallas_call(
        _moba_kernel,
        out_shape=jax.ShapeDtypeStruct((B, S, GROUP_W), BF16),
        grid=(B, MOBA_HEADS, S // BLK),
        in_specs=[
            pl.BlockSpec((1, BLK, HEAD_DIM), lambda b, h, i: (b, i, COL_MQ // HEAD_DIM + h)),
            full(COL_MK), full(COL_MV), tab, tab,
        ],
        out_specs=pl.BlockSpec((1, BLK, HEAD_DIM), lambda b, h, i: (b, i, h)),
        scratch_shapes=[
            pltpu.VMEM((S, HEAD_DIM), BF16),
            pltpu.VMEM((S, HEAD_DIM), BF16),
            pltpu.VMEM((S // BLK, HEAD_DIM), F32),
        ],
        compiler_params=_params(("parallel", "parallel", "arbitrary")),
        name="moba",
    )(P, P, P, cos, sin)


SB_TILE = 256


SB_GROUP = 4


def _sb_kernel(q_ref, k_ref, v_ref, o_ref, qb_ref, run_ref, acc_ref):
    i = pl.program_id(2)
    T = SB_TILE
    qb_ref[...] = q_ref[0].astype(BF16)
    srow = lax.broadcasted_iota(jnp.int32, (T, T), 0)
    scol = lax.broadcasted_iota(jnp.int32, (T, T), 1)
    upper = jnp.where(srow > scol, 1.0, 0.0).astype(BF16)
    strict = scol < srow

    def tile(j, diagonal):
        rows = pl.ds(pl.multiple_of(j * T, T), T)
        heads = range(SB_GROUP)
        cols = [slice(h * HEAD_DIM, (h + 1) * HEAD_DIM) for h in heads]
        z = [lax.dot_general(qb_ref[:, cols[h]], k_ref[0, rows, cols[h]].astype(BF16), (((1,), (1,)), ((), ())),
                             preferred_element_type=F32) * (HEAD_DIM ** -0.5) for h in heads]
        soft = [jnp.log(1.0 + jnp.exp(-jnp.abs(z[h]))) for h in heads]
        log_beta = [jnp.minimum(z[h], 0.0) - soft[h] for h in heads]
        log_keep = [jnp.minimum(-z[h], 0.0) - soft[h] for h in heads]
        if diagonal:
            log_keep = [jnp.where(strict, log_keep[h], 0.0) for h in heads]
        hi = [log_keep[h].astype(BF16) for h in heads]
        lo = [(log_keep[h] - hi[h].astype(F32)).astype(BF16) for h in heads]
        suffix = [jnp.dot(hi[h], upper, preferred_element_type=F32)
                  + jnp.dot(lo[h], upper, preferred_element_type=F32) for h in heads]
        total = [jnp.sum(log_keep[h], axis=-1, keepdims=True) for h in heads]
        if diagonal:
            w = [jnp.where(strict, jnp.exp(log_beta[h] + suffix[h]), 0.0) for h in heads]
        else:
            w = [jnp.exp(log_beta[h] + (suffix[h] + run_ref[h])) for h in heads]
        pv = [jnp.dot(w[h].astype(BF16), v_ref[0, rows, cols[h]].astype(BF16), preferred_element_type=F32)
              for h in heads]
        for h in heads:
            if diagonal:
                acc_ref[h] = pv[h]
                run_ref[h] = total[h]
            else:
                acc_ref[h] += pv[h]
                run_ref[h] += total[h]

    tile(i, True)

    def body(t, carry):
        tile(i - 1 - t, False)
        return carry

    lax.fori_loop(0, i, body, 0)
    for h in range(SB_GROUP):
        o_ref[0, :, h * HEAD_DIM:(h + 1) * HEAD_DIM] = acc_ref[h].astype(o_ref.dtype)


def sb_mixer(P):
    B, S, _ = P.shape
    T = SB_TILE
    W = SB_GROUP * HEAD_DIM
    full = lambda base: pl.BlockSpec((1, S, W), lambda b, h, i: (b, 0, base // W + h))
    return pl.pallas_call(
        _sb_kernel,
        out_shape=jax.ShapeDtypeStruct((B, S, GROUP_W), BF16),
        grid=(B, SB_HEADS // SB_GROUP, S // T),
        in_specs=[
            pl.BlockSpec((1, T, W), lambda b, h, i: (b, i, COL_BQ // W + h)),
            full(COL_BK), full(COL_BV),
        ],
        out_specs=pl.BlockSpec((1, T, W), lambda b, h, i: (b, i, h)),
        scratch_shapes=[
            pltpu.VMEM((T, W), BF16),
            pltpu.VMEM((SB_GROUP, T, 1), F32),
            pltpu.VMEM((SB_GROUP, T, HEAD_DIM), F32),
        ],
        compiler_params=_params(("parallel", "parallel", "arbitrary")),
        name="stick_breaking",
    )(P, P, P)


CAST_ROWS = 256
CAST_BLOCK_BYTES = 6 * 1024 * 1024


def _cast_kernel(w_ref, o_ref):
    o_ref[...] = w_ref[...].astype(o_ref.dtype)


def cast_bf16(w, layer):
    _, R, C = w.shape
    tr = min(CAST_ROWS, R)
    tc = next(C // k for k in range(1, C // LANES + 1)
              if C % k == 0 and (C // k) % LANES == 0 and tr * (C // k) * 4 <= CAST_BLOCK_BYTES)
    return pl.pallas_call(
        _cast_kernel,
        out_shape=jax.ShapeDtypeStruct((R, C), BF16),
        grid=(R // tr, C // tc),
        in_specs=[pl.BlockSpec((None, tr, tc), lambda i, j: (layer, i, j))],
        out_specs=pl.BlockSpec((tr, tc), lambda i, j: (i, j)),
        compiler_params=_params(("parallel", "parallel")),
        name="cast_bf16",
    )(w)


RELAYOUT_ROWS = 64


def _relayout_kernel(w_ref, o_ref):
    lr0 = COL_GG + GROUP_W
    lr1 = lr0 + GLA_GATE_RANK
    w = w_ref[...]
    o_ref[:, :lr0] = w[:, :lr0].astype(BF16)
    o_ref[:, lr0:COL_LR] = w[:, lr1:].astype(BF16)
    o_ref[:, COL_LR:COL_LR + LANES] = jnp.concatenate(
        [w[:, lr0:lr1], jnp.zeros((w.shape[0], LANES - GLA_GATE_RANK), F32)], axis=1).astype(BF16)
    o_ref[:, COL_LR + LANES:] = jnp.zeros((w.shape[0], NP - COL_LR - LANES), BF16)


def relayout_w_in(w, layer):
    _, R, C = w.shape
    tr = min(RELAYOUT_ROWS, R)
    return pl.pallas_call(
        _relayout_kernel,
        out_shape=jax.ShapeDtypeStruct((R, NP), BF16),
        grid=(R // tr,),
        in_specs=[pl.BlockSpec((None, tr, C), lambda i: (layer, i, 0))],
        out_specs=pl.BlockSpec((tr, NP), lambda i: (i, 0)),
        compiler_params=_params(("parallel",)),
        name="relayout_w_in",
    )(w)


def token_mixing(x2, B, S, cos, sin, w_in_r, wg_pad, bg, ng, sinks, w_out_b, ln_g, ln_b):
    P = in_projection(x2, w_in_r).reshape(B, S, NP)
    ys = (gla_mixer(P, wg_pad, bg, ng), swa_mixer(P, cos, sin, sinks), moba_mixer(P, cos, sin), sb_mixer(P))
    ys = [y.reshape(B * S, GROUP_W) for y in ys]
    return out_projection(x2, ys, w_out_b, ln_g, ln_b)


def kernel(x, positions, w_in, gla_w_gate_up, gla_b_gate_up, gla_norm_g, swa_sinks, w_out,
           ffn1_w_gu, ffn1_w_down, ffn2_w_gu, ffn2_w_down, ln_g, ln_b):
    B, S, D = x.shape
    cos, sin = rope_tables(positions)
    x2 = x.reshape(B * S, D)
    for l in range(DEPTH):
        g = ln_g[l].reshape(3, 1, D)
        b = ln_b[l].reshape(3, 1, D)
        x2 = ffn_sublayer(x2, cast_bf16(ffn1_w_gu, l), cast_bf16(ffn1_w_down, l), g[0], b[0])
        wg_pad = jnp.zeros((LANES, GLA_KEY), F32).at[:GLA_GATE_RANK].set(gla_w_gate_up[l])
        x2 = token_mixing(x2, B, S, cos, sin, relayout_w_in(w_in, l), wg_pad,
                          gla_b_gate_up[l].reshape(1, GLA_KEY), gla_norm_g[l].reshape(1, GLA_DV),
                          swa_sinks[l], cast_bf16(w_out, l), g[1], b[1])
        x2 = ffn_sublayer(x2, cast_bf16(ffn2_w_gu, l), cast_bf16(ffn2_w_down, l), g[2], b[2])
    return x2.reshape(B, S, D)
```

```python
import functools

import jax
import jax.numpy as jnp
from jax import lax
from jax.experimental import pallas as pl
from jax.experimental.pallas import tpu as pltpu

F32 = jnp.float32
BF16 = jnp.bfloat16

D_MODEL = 4096
DEPTH = 2
GROUP_W = 1024
HEAD_DIM = 128
GLA_HEADS = 4
GLA_DV = 256
GLA_DK = 128
GLA_KEY = 512
GLA_GATE_RANK = 16
GLA_GATE_NORMALIZER = 16.0
GLA_CHUNK = 64
SWA_HEADS = 8
SWA_KV_HEADS = 2
SWA_GROUP = SWA_HEADS // SWA_KV_HEADS
SWA_WINDOW = 128
SWA_BLOCK = 128
MOBA_HEADS = 8
MOBA_BLOCK = 256
MOBA_TOPK = 3
MOBA_GROUP = 2
SB_HEADS = 8
ROPE_THETA = 10000.0
D_FF = 11008
FFN_RES = 0.5
LN_EPS = 1e-5
RMS_EPS = 1e-5
DN_ALPHA = (2 * DEPTH) ** 0.25

LANES = 128
VMEM_LIMIT = 56 * 1024 * 1024

COL_GQ, COL_GK, COL_GV, COL_GG = 0, 512, 1024, 2048
COL_SQ, COL_SK, COL_SV = 3072, 4096, 4352
COL_MQ, COL_MK, COL_MV = 4608, 5632, 6656
COL_BQ, COL_BK, COL_BV = 7680, 8704, 9728
COL_LR = 10752
PROJ_TN = 1024
NP = 11264

NEG_INF = float("-inf")


def _params(sem, vmem=VMEM_LIMIT):
    return pltpu.CompilerParams(dimension_semantics=sem, vmem_limit_bytes=vmem)


def _layer_norm_rows(y, g, b):
    mu = jnp.mean(y, axis=-1, keepdims=True)
    yc = y - mu
    var = jnp.mean(yc * yc, axis=-1, keepdims=True)
    return yc * lax.rsqrt(var + LN_EPS) * g + b


LN_ROWS = 32
ACC_COLS = 512


def _residual_layer_norm(o_ref, x_ref, res_scale, g_ref, b_ref):
    def body(r, carry):
        rows = pl.ds(pl.multiple_of(r * LN_ROWS, LN_ROWS), LN_ROWS)
        y = o_ref[rows, :]
        if x_ref is not None:
            y = DN_ALPHA * x_ref[rows, :] + res_scale * y
        o_ref[rows, :] = _layer_norm_rows(y, g_ref[...], b_ref[...])
        return carry

    lax.fori_loop(0, o_ref.shape[0] // LN_ROWS, body, 0)


def _accumulate_dot(o_ref, a, w_ref):
    for c in range(0, o_ref.shape[1], ACC_COLS):
        o_ref[:, c:c + ACC_COLS] += jnp.dot(a, w_ref[:, c:c + ACC_COLS], preferred_element_type=F32)


def _rope_kernel(pos_ref, inv_ref, sign_ref, cos_ref, sin_ref):
    ang = pos_ref[0].astype(F32) * inv_ref[...]
    cos_ref[0] = jnp.cos(ang)
    sin_ref[0] = jnp.sin(ang) * sign_ref[...]


def rope_tables(positions):
    B, S = positions.shape
    ts = min(S, 512)
    inv = 1.0 / (ROPE_THETA ** (jnp.arange(0, HEAD_DIM, 2, dtype=F32) / HEAD_DIM))
    inv_full = jnp.concatenate([inv, inv]).reshape(1, HEAD_DIM)
    sign = jnp.concatenate([-jnp.ones((HEAD_DIM // 2,), F32), jnp.ones((HEAD_DIM // 2,), F32)]).reshape(1, HEAD_DIM)
    pos_b = jnp.broadcast_to(positions[:, :, None], (B, S, HEAD_DIM))
    blk = pl.BlockSpec((1, ts, HEAD_DIM), lambda b, s: (b, s, 0))
    vec = pl.BlockSpec((1, HEAD_DIM), lambda b, s: (0, 0))
    return pl.pallas_call(
        _rope_kernel,
        out_shape=(jax.ShapeDtypeStruct((B, S, HEAD_DIM), F32),) * 2,
        grid=(B, S // ts),
        in_specs=[blk, vec, vec],
        out_specs=(blk, blk),
        compiler_params=_params(("parallel", "parallel")),
        name="rope_tables",
    )(pos_b, inv_full, sign)


def _rope(x, cos, sin):
    return x * cos + pltpu.roll(x, HEAD_DIM // 2, axis=1) * sin


FFN_TM = 512
FFN_TF = 256


def _ffn_kernel(x_ref, wg_ref, wu_ref, wd_ref, g_ref, b_ref, o_ref, xb_ref):
    j = pl.program_id(1)

    @pl.when(j == 0)
    def _():
        xb_ref[...] = x_ref[...].astype(BF16)
        o_ref[...] = jnp.zeros_like(o_ref)

    xb = xb_ref[...]
    gate = jnp.dot(xb, wg_ref[...], preferred_element_type=F32)
    up = jnp.dot(xb, wu_ref[...], preferred_element_type=F32)
    act = (gate / (1.0 + jnp.exp(-gate)) * up).astype(BF16)
    _accumulate_dot(o_ref, act, wd_ref)

    @pl.when(j == pl.num_programs(1) - 1)
    def _():
        _residual_layer_norm(o_ref, x_ref, FFN_RES, g_ref, b_ref)


def ffn_sublayer(x, w_gu, w_down, ln_g, ln_b):
    T, D = x.shape
    tm = min(FFN_TM, T)
    nj = D_FF // FFN_TF
    vec = pl.BlockSpec((1, D), lambda i, j: (0, 0))
    return pl.pallas_call(
        _ffn_kernel,
        out_shape=jax.ShapeDtypeStruct((T, D), F32),
        grid=(T // tm, nj),
        in_specs=[
            pl.BlockSpec((tm, D), lambda i, j: (i, 0)),
            pl.BlockSpec((D, FFN_TF), lambda i, j: (0, j)),
            pl.BlockSpec((D, FFN_TF), lambda i, j: (0, j + nj)),
            pl.BlockSpec((FFN_TF, D), lambda i, j: (j, 0)),
            vec, vec,
        ],
        out_specs=pl.BlockSpec((tm, D), lambda i, j: (i, 0)),
        scratch_shapes=[pltpu.VMEM((tm, D), BF16)],
        compiler_params=_params(("parallel", "arbitrary")),
        name="ffn",
    )(x, w_gu, w_gu, w_down, ln_g, ln_b)


PROJ_TM = 512


def _proj_kernel(x_ref, w_ref, o_ref, xb_ref):
    @pl.when(pl.program_id(1) == 0)
    def _():
        xb_ref[...] = x_ref[...].astype(BF16)

    o_ref[...] = jnp.dot(xb_ref[...], w_ref[...], preferred_element_type=F32)


def in_projection(x, w_in_r):
    T, D = x.shape
    tm = min(PROJ_TM, T)
    return pl.pallas_call(
        _proj_kernel,
        out_shape=jax.ShapeDtypeStruct((T, NP), F32),
        grid=(T // tm, NP // PROJ_TN),
        in_specs=[
            pl.BlockSpec((tm, D), lambda i, j: (i, 0)),
            pl.BlockSpec((D, PROJ_TN), lambda i, j: (0, j)),
        ],
        out_specs=pl.BlockSpec((tm, PROJ_TN), lambda i, j: (i, j)),
        scratch_shapes=[pltpu.VMEM((tm, D), BF16)],
        compiler_params=_params(("parallel", "arbitrary")),
        name="in_proj",
    )(x, w_in_r)


OUT_TM = 256


def _outproj_kernel(x_ref, y0_ref, y1_ref, y2_ref, y3_ref, w_ref, g_ref, b_ref, o_ref):
    for c in range(0, o_ref.shape[1], ACC_COLS):
        cols = slice(c, c + ACC_COLS)
        acc = DN_ALPHA * x_ref[:, cols]
        for idx, y_ref in enumerate((y0_ref, y1_ref, y2_ref, y3_ref)):
            acc = acc + jnp.dot(y_ref[...], w_ref[idx * GROUP_W:(idx + 1) * GROUP_W, cols],
                                preferred_element_type=F32)
        o_ref[:, cols] = acc
    _residual_layer_norm(o_ref, None, 1.0, g_ref, b_ref)


def out_projection(x, ys, w_out, ln_g, ln_b):
    T, D = x.shape
    tm = min(OUT_TM, T)
    vec = pl.BlockSpec((1, D), lambda i: (0, 0))
    yspec = pl.BlockSpec((tm, GROUP_W), lambda i: (i, 0))
    return pl.pallas_call(
        _outproj_kernel,
        out_shape=jax.ShapeDtypeStruct((T, D), F32),
        grid=(T // tm,),
        in_specs=[
            pl.BlockSpec((tm, D), lambda i: (i, 0)),
            yspec, yspec, yspec, yspec,
            pl.BlockSpec(w_out.shape, lambda i: (0, 0), pipeline_mode=pl.Buffered(1)),
            vec, vec,
        ],
        out_specs=pl.BlockSpec((tm, D), lambda i: (i, 0)),
        compiler_params=_params(("parallel",)),
        name="out_proj",
    )(x, *ys, w_out, ln_g, ln_b)


GLA_ROWS = 512
SUBLANES = 8
GLA_SUB = 16
GLA_GROUP = 4


def _cumsum_rows(x):
    n = x.shape[0]
    row = lax.broadcasted_iota(jnp.int32, x.shape, 0)
    sh = 1
    while sh < n:
        x = x + jnp.where(row >= sh, pltpu.roll(x, sh, axis=0), 0.0)
        sh *= 2
    return x


def _gla_kernel(q_ref, k_ref, v_ref, gg_ref, lr_ref, wg_ref, bg_ref, ng_ref, o_ref,
                st_ref, b_scr, k_scr, v_scr):
    C = GLA_CHUNK
    heads = range(GLA_GROUP)
    kcols = [slice(h * GLA_DK, (h + 1) * GLA_DK) for h in heads]
    vcols = [slice(h * GLA_DV, (h + 1) * GLA_DV) for h in heads]
    nt = (((1,), (1,)), ((), ()))

    @pl.when(pl.program_id(2) == 0)
    def _():
        st_ref[...] = jnp.zeros_like(st_ref)

    n_chunks = q_ref.shape[1] // C
    row8 = lax.broadcasted_iota(jnp.int32, (SUBLANES, 1), 0)

    def chunk(c, carry):
        r0 = pl.multiple_of(c * C, C)
        rows = pl.ds(r0, C)
        lr = lr_ref[0, rows, :]
        logits = [jnp.dot(lr, wg_ref[:, kcols[h]], preferred_element_type=F32,
                          precision=lax.Precision.HIGHEST) + bg_ref[:, kcols[h]] for h in heads]
        q = [q_ref[0, rows, kcols[h]] * (GLA_DK ** -0.5) for h in heads]
        k = [k_ref[0, rows, kcols[h]] for h in heads]
        v = [v_ref[0, rows, vcols[h]] for h in heads]
        g = [(jnp.minimum(x, 0.0) - jnp.log(1.0 + jnp.exp(-jnp.abs(x)))) / GLA_GATE_NORMALIZER for x in logits]
        b = [_cumsum_rows(x) for x in g]
        b_last = [x[C - 1:C, :] for x in b]
        for h in heads:
            b_scr[h] = b[h]
            k_scr[h] = k[h]
            v_scr[h] = v[h]
        st = [st_ref[h] for h in heads]
        o_inter = [lax.dot_general((q[h] * jnp.exp(b[h])).astype(BF16), st[h].astype(BF16), nt,
                                   preferred_element_type=F32) for h in heads]
        v_b = [x.astype(BF16) for x in v]
        for h in heads:
            kd = (k[h] * jnp.exp(b_last[h] - b[h])).astype(BF16)
            st_ref[h] = st[h] * jnp.exp(b_last[h]) + lax.dot_general(
                v_b[h], kd, (((0,), (0,)), ((), ())), preferred_element_type=F32)

        for s in range(0, C, GLA_SUB):
            blk_acc = [o_inter[h][s:s + GLA_SUB] for h in heads]
            if s > 0:
                att = []
                for h in heads:
                    b_s = b_scr[h, s:s + 1, :]
                    q_t = (q[h][s:s + GLA_SUB] * jnp.exp(b[h][s:s + GLA_SUB] - b_s)).astype(BF16)
                    k_t = jnp.concatenate([k[h][:s] * jnp.exp(b_s - b[h][:s]), jnp.zeros((C - s, GLA_DK), F32)],
                                          axis=0).astype(BF16)
                    att.append(lax.dot_general(q_t, k_t, nt, preferred_element_type=F32))
                blk_acc = [blk_acc[h] + jnp.dot(att[h].astype(BF16), v_b[h], preferred_element_type=F32)
                           for h in heads]
            for i0 in range(s, s + GLA_SUB, SUBLANES):
                for h in heads:
                    q_i = q[h][i0:i0 + SUBLANES]
                    b_i = b[h][i0:i0 + SUBLANES]
                    acc = blk_acc[h][i0 - s:i0 - s + SUBLANES]
                    for j in range(s, i0 + SUBLANES):
                        b_j = b_scr[h, j:j + 1, :]
                        k_j = k_scr[h, j:j + 1, :]
                        v_j = v_scr[h, j:j + 1, :]
                        e = jnp.exp(jnp.minimum(b_i - b_j, 0.0))
                        a = jnp.sum(q_i * k_j * e, axis=-1, keepdims=True)
                        if j >= i0:
                            a = jnp.where(row8 >= (j - i0), a, 0.0)
                        acc = acc + a * v_j
                    acc = acc * lax.rsqrt(jnp.mean(acc * acc, axis=-1, keepdims=True) + RMS_EPS) * ng_ref[...]
                    gg = gg_ref[0, pl.ds(r0 + i0, SUBLANES), vcols[h]]
                    o_ref[0, pl.ds(r0 + i0, SUBLANES), vcols[h]] = (
                        acc * (gg / (1.0 + jnp.exp(-gg)))).astype(o_ref.dtype)
        return carry

    lax.fori_loop(0, n_chunks, chunk, 0)


def gla_mixer(P, wg_pad, bg, ng):
    B, S, _ = P.shape
    R = min(GLA_ROWS, S)
    kw = GLA_GROUP * GLA_DK
    vw = GLA_GROUP * GLA_DV
    kb = lambda base: (lambda b, h, r: (b, r, base // kw + h))
    vb = lambda base: (lambda b, h, r: (b, r, base // vw + h))
    return pl.pallas_call(
        _gla_kernel,
        out_shape=jax.ShapeDtypeStruct((B, S, GROUP_W), BF16),
        grid=(B, GLA_HEADS // GLA_GROUP, S // R),
        in_specs=[
            pl.BlockSpec((1, R, kw), kb(COL_GQ)),
            pl.BlockSpec((1, R, kw), kb(COL_GK)),
            pl.BlockSpec((1, R, vw), vb(COL_GV)),
            pl.BlockSpec((1, R, vw), vb(COL_GG)),
            pl.BlockSpec((1, R, LANES), lambda b, h, r: (b, r, COL_LR // LANES)),
            pl.BlockSpec((LANES, kw), lambda b, h, r: (0, h)),
            pl.BlockSpec((1, kw), lambda b, h, r: (0, h)),
            pl.BlockSpec((1, GLA_DV), lambda b, h, r: (0, 0)),
        ],
        out_specs=pl.BlockSpec((1, R, vw), lambda b, h, r: (b, r, h)),
        scratch_shapes=[
            pltpu.VMEM((GLA_GROUP, GLA_DV, GLA_DK), F32),
            pltpu.VMEM((GLA_GROUP, GLA_CHUNK, GLA_DK), F32),
            pltpu.VMEM((GLA_GROUP, GLA_CHUNK, GLA_DK), F32),
            pltpu.VMEM((GLA_GROUP, GLA_CHUNK, GLA_DV), F32),
        ],
        compiler_params=_params(("parallel", "parallel", "arbitrary")),
        name="gla",
    )(P, P, P, P, P, wg_pad, bg, ng)


def _swa_kernel(sink_ref, q_ref, kc_ref, kp_ref, vc_ref, vp_ref, cc_ref, sc_ref, cp_ref, sp_ref, o_ref):
    n = pl.program_id(1)
    BLK = SWA_BLOCK
    nt = (((1,), (1,)), ((), ()))
    cos, sin = cc_ref[0], sc_ref[0]
    cos_p, sin_p = cp_ref[0], sp_ref[0]
    qpos = BLK + lax.broadcasted_iota(jnp.int32, (BLK, 2 * BLK), 0)
    kpos = lax.broadcasted_iota(jnp.int32, (BLK, 2 * BLK), 1)
    rel = qpos - kpos
    mask = (rel >= 0) & (rel < SWA_WINDOW) & ((kpos >= BLK) | (n > 0))
    kvs = range(SWA_KV_HEADS)
    hd = lambda h: slice(h * HEAD_DIM, (h + 1) * HEAD_DIM)
    k_cat = [jnp.concatenate([_rope(kp_ref[0, :, hd(kv)], cos_p, sin_p), _rope(kc_ref[0, :, hd(kv)], cos, sin)],
                             axis=0).astype(BF16) for kv in kvs]
    v_cat = [jnp.concatenate([vp_ref[0, :, hd(kv)], vc_ref[0, :, hd(kv)]], axis=0).astype(BF16) for kv in kvs]
    heads = range(SWA_HEADS)
    q = [_rope(q_ref[0, :, hd(h)], cos, sin).astype(BF16) for h in heads]
    s = [lax.dot_general(q[h], k_cat[h // SWA_GROUP], nt, preferred_element_type=F32) * (HEAD_DIM ** -0.5)
         for h in heads]
    s = [jnp.where(mask, x, NEG_INF) for x in s]
    sink = [sink_ref[h] for h in heads]
    m = [jnp.maximum(jnp.max(s[h], axis=-1, keepdims=True), sink[h]) for h in heads]
    p = [jnp.exp(s[h] - m[h]) for h in heads]
    den = [jnp.sum(p[h], axis=-1, keepdims=True) + jnp.exp(sink[h] - m[h]) for h in heads]
    o = [jnp.dot(p[h].astype(BF16), v_cat[h // SWA_GROUP], preferred_element_type=F32) / den[h] for h in heads]
    for h in heads:
        o_ref[0, :, hd(h)] = o[h].astype(o_ref.dtype)


def swa_mixer(P, cos, sin, sinks):
    B, S, _ = P.shape
    BLK = SWA_BLOCK
    kvw = SWA_KV_HEADS * HEAD_DIM
    cur = lambda base, w: (lambda b, n, s: (b, n, base // w))
    prev = lambda base, w: (lambda b, n, s: (b, jnp.maximum(n - 1, 0), base // w))
    kvb = lambda f: pl.BlockSpec((1, BLK, kvw), f)
    tab = lambda f: pl.BlockSpec((1, BLK, HEAD_DIM), f)
    return pl.pallas_call(
        _swa_kernel,
        out_shape=jax.ShapeDtypeStruct((B, S, GROUP_W), BF16),
        grid_spec=pltpu.PrefetchScalarGridSpec(
            num_scalar_prefetch=1,
            grid=(B, S // BLK),
            in_specs=[
                pl.BlockSpec((1, BLK, GROUP_W), cur(COL_SQ, GROUP_W)),
                kvb(cur(COL_SK, kvw)), kvb(prev(COL_SK, kvw)), kvb(cur(COL_SV, kvw)), kvb(prev(COL_SV, kvw)),
                tab(cur(0, HEAD_DIM)), tab(cur(0, HEAD_DIM)), tab(prev(0, HEAD_DIM)), tab(prev(0, HEAD_DIM)),
            ],
            out_specs=pl.BlockSpec((1, BLK, GROUP_W), lambda b, n, s: (b, n, 0)),
        ),
        compiler_params=_params(("parallel", "arbitrary")),
        name="swa",
    )(sinks, P, P, P, P, P, cos, sin, cos, sin)


def _moba_kernel(q_ref, k_ref, v_ref, cos_ref, sin_ref, o_ref, kr_ref, vb_ref, kbar_ref):
    i = pl.program_id(2)
    BLK = MOBA_BLOCK
    S = k_ref.shape[1]
    NB = S // BLK
    heads = range(MOBA_GROUP)
    cols = [slice(h * HEAD_DIM, (h + 1) * HEAD_DIM) for h in heads]
    nt = (((1,), (1,)), ((), ()))

    @pl.when(i == 0)
    def _():
        for h in heads:
            kr = _rope(k_ref[0, :, cols[h]], cos_ref[0], sin_ref[0])
            kr_ref[:, cols[h]] = kr.astype(BF16)
            vb_ref[:, cols[h]] = v_ref[0, :, cols[h]].astype(BF16)
            for n in range(NB):
                kbar_ref[h, n:n + 1, :] = jnp.mean(kr[n * BLK:(n + 1) * BLK], axis=0, keepdims=True)

    rows = pl.ds(pl.multiple_of(i * BLK, BLK), BLK)
    cos_q, sin_q = cos_ref[0, rows, :], sin_ref[0, rows, :]
    q = [_rope(q_ref[0, :, cols[h]], cos_q, sin_q) for h in heads]

    gate = [lax.dot_general(kbar_ref[h], q[h], nt, preferred_element_type=F32,
                            precision=lax.Precision.HIGHEST) for h in heads]
    blk = lax.broadcasted_iota(jnp.int32, (NB, BLK), 0)
    past = blk < i
    sel = []
    for h in heads:
        sel_h = []
        for n in range(NB):
            g_n = gate[h][n:n + 1, :]
            ahead = ((gate[h] > g_n) | ((gate[h] == g_n) & (blk < n))) & past
            rank = jnp.sum(jnp.where(ahead, 1.0, 0.0), axis=0, keepdims=True)
            sel_h.append(rank < float(MOBA_TOPK))
        sel.append(sel_h)

    qb = [x.astype(BF16) for x in q]
    key = lax.broadcasted_iota(jnp.int32, (BLK, BLK), 0)
    qry = lax.broadcasted_iota(jnp.int32, (BLK, BLK), 1)
    causal = key <= qry

    def attend(nb):
        blocks = range(nb)
        scores = [[lax.dot_general(kr_ref[n * BLK:(n + 1) * BLK, cols[h]], qb[h], nt,
                                   preferred_element_type=F32) for n in blocks] for h in heads]
        masked = [[jnp.where((sel[h][n] & (n < i)) | (causal & (n == i)),
                             scores[h][n] * (HEAD_DIM ** -0.5), NEG_INF) for n in blocks] for h in heads]
        m = [functools.reduce(jnp.maximum, [jnp.max(x, axis=0, keepdims=True) for x in masked[h]]) for h in heads]
        probs = [[jnp.exp(x - m[h]) for x in masked[h]] for h in heads]
        den = [functools.reduce(jnp.add, [jnp.sum(p, axis=0, keepdims=True) for p in probs[h]]) for h in heads]
        outs = [[lax.dot_general(probs[h][n].astype(BF16), vb_ref[n * BLK:(n + 1) * BLK, cols[h]],
                                 (((0,), (0,)), ((), ())), preferred_element_type=F32) for n in blocks]
                for h in heads]
        for h in heads:
            inv = jnp.transpose(jnp.broadcast_to(1.0 / den[h], (HEAD_DIM, BLK)))
            o_ref[0, :, cols[h]] = (functools.reduce(jnp.add, outs[h]) * inv).astype(o_ref.dtype)

    for nb in range(2, NB + 1, 2):
        @pl.when((i >= nb - 2) & (i < nb))
        def _(nb=nb):
            attend(nb)


def moba_mixer(P, cos, sin):
    B, S, _ = P.shape
    BLK = MOBA_BLOCK
    W = MOBA_GROUP * HEAD_DIM
    full = lambda base: pl.BlockSpec((1, S, W), lambda b, h, i: (b, 0, base // W + h))
    tab = pl.BlockSpec((1, S, HEAD_DIM), lambda b, h, i: (b, 0, 0))
    return pl.pallas_call(
        _moba_kernel,
        out_shape=jax.ShapeDtypeStruct((B, S, GROUP_W), BF16),
        grid=(B, MOBA_HEADS // MOBA_GROUP, S // BLK),
        in_specs=[
            pl.BlockSpec((1, BLK, W), lambda b, h, i: (b, i, COL_MQ // W + h)),
            full(COL_MK), full(COL_MV), tab, tab,
        ],
        out_specs=pl.BlockSpec((1, BLK, W), lambda b, h, i: (b, i, h)),
        scratch_shapes=[
            pltpu.VMEM((S, W), BF16),
            pltpu.VMEM((S, W), BF16),
            pltpu.VMEM((MOBA_GROUP, S // BLK, HEAD_DIM), F32),
        ],
        compiler_params=_params(("parallel", "parallel", "arbitrary")),
        name="moba",
    )(P, P, P, cos, sin)


SB_TILE = 256
SB_GROUP = 4


def _sb_kernel(q_ref, k_ref, v_ref, o_ref, qb_ref, run_ref, acc_ref):
    i = pl.program_id(2)
    T = SB_TILE
    qb_ref[...] = (q_ref[0] * (HEAD_DIM ** -0.5)).astype(BF16)
    srow = lax.broadcasted_iota(jnp.int32, (T, T), 0)
    scol = lax.broadcasted_iota(jnp.int32, (T, T), 1)
    upper = jnp.where(srow > scol, 1.0, 0.0).astype(BF16)
    strict = scol < srow

    def tile(j, diagonal):
        rows = pl.ds(pl.multiple_of(j * T, T), T)
        heads = range(SB_GROUP)
        cols = [slice(h * HEAD_DIM, (h + 1) * HEAD_DIM) for h in heads]
        z = [lax.dot_general(qb_ref[:, cols[h]], k_ref[0, rows, cols[h]].astype(BF16), (((1,), (1,)), ((), ())),
                             preferred_element_type=F32) for h in heads]
        soft = [jnp.log(1.0 + jnp.exp(-jnp.abs(z[h]))) for h in heads]
        log_keep = [-jnp.maximum(z[h], 0.0) - soft[h] for h in heads]
        log_beta = [log_keep[h] + z[h] for h in heads]
        if diagonal:
            log_keep = [jnp.where(strict, log_keep[h], 0.0) for h in heads]
        hi = [log_keep[h].astype(BF16) for h in heads]
        lo = [(log_keep[h] - hi[h].astype(F32)).astype(BF16) for h in heads]
        suffix = [jnp.dot(hi[h], upper, preferred_element_type=F32)
                  + jnp.dot(lo[h], upper, preferred_element_type=F32) for h in heads]
        total = [suffix[h][:, 0:1] + log_keep[h][:, 0:1] for h in heads]
        if diagonal:
            w = [jnp.where(strict, jnp.exp(log_beta[h] + suffix[h]), 0.0) for h in heads]
        else:
            w = [jnp.exp(log_beta[h] + (suffix[h] + run_ref[h])) for h in heads]
        pv = [jnp.dot(w[h].astype(BF16), v_ref[0, rows, cols[h]].astype(BF16), preferred_element_type=F32)
              for h in heads]
        for h in heads:
            if diagonal:
                acc_ref[h] = pv[h]
                run_ref[h] = total[h]
            else:
                acc_ref[h] += pv[h]
                run_ref[h] += total[h]

    tile(i, True)

    def body(t, carry):
        tile(i - 1 - t, False)
        return carry

    lax.fori_loop(0, i, body, 0)
    for h in range(SB_GROUP):
        o_ref[0, :, h * HEAD_DIM:(h + 1) * HEAD_DIM] = acc_ref[h].astype(o_ref.dtype)


def sb_mixer(P):
    B, S, _ = P.shape
    T = SB_TILE
    W = SB_GROUP * HEAD_DIM
    full = lambda base: pl.BlockSpec((1, S, W), lambda b, h, i: (b, 0, base // W + h))
    return pl.pallas_call(
        _sb_kernel,
        out_shape=jax.ShapeDtypeStruct((B, S, GROUP_W), BF16),
        grid=(B, SB_HEADS // SB_GROUP, S // T),
        in_specs=[
            pl.BlockSpec((1, T, W), lambda b, h, i: (b, i, COL_BQ // W + h)),
            full(COL_BK), full(COL_BV),
        ],
        out_specs=pl.BlockSpec((1, T, W), lambda b, h, i: (b, i, h)),
        scratch_shapes=[
            pltpu.VMEM((T, W), BF16),
            pltpu.VMEM((SB_GROUP, T, 1), F32),
            pltpu.VMEM((SB_GROUP, T, HEAD_DIM), F32),
        ],
        compiler_params=_params(("parallel", "parallel", "arbitrary")),
        name="stick_breaking",
    )(P, P, P)


CAST_ROWS = 256
CAST_BLOCK_BYTES = 6 * 1024 * 1024


def _cast_kernel(w_ref, o_ref):
    o_ref[...] = w_ref[...].astype(o_ref.dtype)


def cast_bf16(w, layer):
    _, R, C = w.shape
    tr = min(CAST_ROWS, R)
    tc = next(C // k for k in range(1, C // LANES + 1)
              if C % k == 0 and (C // k) % LANES == 0 and tr * (C // k) * 4 <= CAST_BLOCK_BYTES)
    return pl.pallas_call(
        _cast_kernel,
        out_shape=jax.ShapeDtypeStruct((R, C), BF16),
        grid=(R // tr, C // tc),
        in_specs=[pl.BlockSpec((None, tr, tc), lambda i, j: (layer, i, j))],
        out_specs=pl.BlockSpec((tr, tc), lambda i, j: (i, j)),
        compiler_params=_params(("parallel", "parallel")),
        name="cast_bf16",
    )(w)


RELAYOUT_COLS = 256


def _relayout_kernel(a_ref, b_ref, o_ref):
    ob = pl.program_id(0)
    lr_blk = (COL_GG + GROUP_W) // RELAYOUT_COLS
    end_blk = COL_LR // RELAYOUT_COLS
    rank = GLA_GATE_RANK

    @pl.when(ob < lr_blk)
    def _():
        o_ref[...] = jnp.transpose(a_ref[...]).astype(BF16)

    @pl.when((ob >= lr_blk) & (ob < end_blk))
    def _():
        w = jnp.concatenate([a_ref[rank:, :], b_ref[:rank, :]], axis=0)
        o_ref[...] = jnp.transpose(w).astype(BF16)

    @pl.when(ob == end_blk)
    def _():
        w = jnp.concatenate([a_ref[:rank, :], jnp.zeros((RELAYOUT_COLS - rank, a_ref.shape[1]), F32)], axis=0)
        o_ref[...] = jnp.transpose(w).astype(BF16)

    @pl.when(ob > end_blk)
    def _():
        o_ref[...] = jnp.zeros(o_ref.shape, BF16)


def relayout_w_in(w, layer):
    _, R, C = w.shape
    wt = jnp.swapaxes(w, 1, 2)
    lr_blk = (COL_GG + GROUP_W) // RELAYOUT_COLS
    end_blk = COL_LR // RELAYOUT_COLS
    blk = lambda f: pl.BlockSpec((None, RELAYOUT_COLS, R), f)
    return pl.pallas_call(
        _relayout_kernel,
        out_shape=jax.ShapeDtypeStruct((R, NP), BF16),
        grid=(NP // RELAYOUT_COLS,),
        in_specs=[
            blk(lambda ob: (layer, jnp.where(ob == end_blk, lr_blk, jnp.minimum(ob, end_blk)), 0)),
            blk(lambda ob: (layer, jnp.minimum(ob + 1, end_blk), 0)),
        ],
        out_specs=pl.BlockSpec((R, RELAYOUT_COLS), lambda ob: (0, ob)),
        compiler_params=_params(("parallel",)),
        name="relayout_w_in",
    )(wt, wt)


def token_mixing(x2, B, S, cos, sin, w_in_r, wg_pad, bg, ng, sinks, w_out_b, ln_g, ln_b):
    P = in_projection(x2, w_in_r).reshape(B, S, NP)
    ys = (gla_mixer(P, wg_pad, bg, ng), swa_mixer(P, cos, sin, sinks), moba_mixer(P, cos, sin), sb_mixer(P))
    ys = [y.reshape(B * S, GROUP_W) for y in ys]
    return out_projection(x2, ys, w_out_b, ln_g, ln_b)


def kernel(x, positions, w_in, gla_w_gate_up, gla_b_gate_up, gla_norm_g, swa_sinks, w_out,
           ffn1_w_gu, ffn1_w_down, ffn2_w_gu, ffn2_w_down, ln_g, ln_b):
    B, S, D = x.shape
    cos, sin = rope_tables(positions)
    x2 = x.reshape(B * S, D)
    for l in range(DEPTH):
        g = ln_g[l].reshape(3, 1, D)
        b = ln_b[l].reshape(3, 1, D)
        x2 = ffn_sublayer(x2, cast_bf16(ffn1_w_gu, l), cast_bf16(ffn1_w_down, l), g[0], b[0])
        wg_pad = jnp.zeros((LANES, GLA_KEY), F32).at[:GLA_GATE_RANK].set(gla_w_gate_up[l])
        x2 = token_mixing(x2, B, S, cos, sin, relayout_w_in(w_in, l), wg_pad,
                          gla_b_gate_up[l].reshape(1, GLA_KEY), gla_norm_g[l].reshape(1, GLA_DV),
                          swa_sinks[l], cast_bf16(w_out, l), g[1], b[1])
        x2 = ffn_sublayer(x2, cast_bf16(ffn2_w_gu, l), cast_bf16(ffn2_w_down, l), g[2], b[2])
    return x2.reshape(B, S, D)
```

```python
import functools

import jax
import jax.numpy as jnp
from jax import lax
from jax.experimental import pallas as pl
from jax.experimental.pallas import tpu as pltpu

F32 = jnp.float32
BF16 = jnp.bfloat16

D_MODEL = 4096
DEPTH = 2
GROUP_W = 1024
HEAD_DIM = 128
GLA_HEADS = 4
GLA_DV = 256
GLA_DK = 128
GLA_KEY = 512
GLA_GATE_RANK = 16
GLA_GATE_NORMALIZER = 16.0
GLA_CHUNK = 64
SWA_HEADS = 8
SWA_KV_HEADS = 2
SWA_GROUP = SWA_HEADS // SWA_KV_HEADS
SWA_WINDOW = 128
SWA_BLOCK = 128
MOBA_HEADS = 8
MOBA_BLOCK = 256
MOBA_TOPK = 3
MOBA_GROUP = 2
SB_HEADS = 8
ROPE_THETA = 10000.0
D_FF = 11008
FFN_RES = 0.5
LN_EPS = 1e-5
RMS_EPS = 1e-5
DN_ALPHA = (2 * DEPTH) ** 0.25

LANES = 128
VMEM_LIMIT = 56 * 1024 * 1024

COL_GQ, COL_GK, COL_GV, COL_GG = 0, 512, 1024, 2048
COL_SQ, COL_SK, COL_SV = 3072, 4096, 4352
COL_MQ, COL_MK, COL_MV = 4608, 5632, 6656
COL_BQ, COL_BK, COL_BV = 7680, 8704, 9728
COL_LR = 10752
PROJ_TN = 1024
NP = 11264

NEG_INF = float("-inf")


def _params(sem, vmem=VMEM_LIMIT):
    return pltpu.CompilerParams(dimension_semantics=sem, vmem_limit_bytes=vmem)


def _layer_norm_rows(y, g, b):
    mu = jnp.mean(y, axis=-1, keepdims=True)
    yc = y - mu
    var = jnp.mean(yc * yc, axis=-1, keepdims=True)
    return yc * lax.rsqrt(var + LN_EPS) * g + b


LN_ROWS = 32
ACC_COLS = 512


def _residual_layer_norm(o_ref, x_ref, res_scale, g_ref, b_ref):
    def body(r, carry):
        rows = pl.ds(pl.multiple_of(r * LN_ROWS, LN_ROWS), LN_ROWS)
        y = o_ref[rows, :]
        if x_ref is not None:
            y = DN_ALPHA * x_ref[rows, :] + res_scale * y
        o_ref[rows, :] = _layer_norm_rows(y, g_ref[...], b_ref[...])
        return carry

    lax.fori_loop(0, o_ref.shape[0] // LN_ROWS, body, 0, unroll=2)


def _accumulate_dot(o_ref, a, w_ref):
    for c in range(0, o_ref.shape[1], ACC_COLS):
        o_ref[:, c:c + ACC_COLS] += jnp.dot(a, w_ref[:, c:c + ACC_COLS], preferred_element_type=F32)


def _rope_kernel(pos_ref, inv_ref, sign_ref, cos_ref, sin_ref):
    ang = pos_ref[0].astype(F32) * inv_ref[...]
    cos_ref[0] = jnp.cos(ang)
    sin_ref[0] = jnp.sin(ang) * sign_ref[...]


def rope_tables(positions):
    B, S = positions.shape
    ts = min(S, 512)
    inv = 1.0 / (ROPE_THETA ** (jnp.arange(0, HEAD_DIM, 2, dtype=F32) / HEAD_DIM))
    inv_full = jnp.concatenate([inv, inv]).reshape(1, HEAD_DIM)
    sign = jnp.concatenate([-jnp.ones((HEAD_DIM // 2,), F32), jnp.ones((HEAD_DIM // 2,), F32)]).reshape(1, HEAD_DIM)
    pos_b = jnp.broadcast_to(positions[:, :, None], (B, S, HEAD_DIM))
    blk = pl.BlockSpec((1, ts, HEAD_DIM), lambda b, s: (b, s, 0))
    vec = pl.BlockSpec((1, HEAD_DIM), lambda b, s: (0, 0))
    return pl.pallas_call(
        _rope_kernel,
        out_shape=(jax.ShapeDtypeStruct((B, S, HEAD_DIM), F32),) * 2,
        grid=(B, S // ts),
        in_specs=[blk, vec, vec],
        out_specs=(blk, blk),
        compiler_params=_params(("parallel", "parallel")),
        name="rope_tables",
    )(pos_b, inv_full, sign)


def _rope(x, cos, sin):
    return x * cos + pltpu.roll(x, HEAD_DIM // 2, axis=1) * sin


FFN_TM = 512
FFN_TF = 256


def _ffn_kernel(x_ref, wg0_ref, wu0_ref, wd0_ref, wg1_ref, wu1_ref, wd1_ref, g_ref, b_ref, o_ref, xb_ref):
    j = pl.program_id(1)
    last = pl.num_programs(1) - 1

    @pl.when(j == 0)
    def _():
        xb_ref[...] = x_ref[...].astype(BF16)
        o_ref[...] = jnp.zeros_like(o_ref)

    def chunk(wg_ref, wu_ref, wd_ref):
        xb = xb_ref[...]
        gate = jnp.dot(xb, wg_ref[...], preferred_element_type=F32)
        up = jnp.dot(xb, wu_ref[...], preferred_element_type=F32)
        act = (gate / (1.0 + jnp.exp(-gate)) * up).astype(BF16)
        _accumulate_dot(o_ref, act, wd_ref)

    @pl.when(j < last)
    def _():
        chunk(wg0_ref, wu0_ref, wd0_ref)
        chunk(wg1_ref, wu1_ref, wd1_ref)

    @pl.when(j == last)
    def _():
        chunk(wg0_ref, wu0_ref, wd0_ref)
        _residual_layer_norm(o_ref, x_ref, FFN_RES, g_ref, b_ref)


def ffn_sublayer(x, w_gu, w_down, ln_g, ln_b):
    T, D = x.shape
    tm = min(FFN_TM, T)
    nj = D_FF // FFN_TF
    steps = (nj + 1) // 2
    first = lambda j: 2 * j
    second = lambda j: jnp.minimum(2 * j + 1, nj - 1)
    vec = pl.BlockSpec((1, D), lambda i, j: (0, 0))
    col = lambda f, off: pl.BlockSpec((D, FFN_TF), lambda i, j: (0, f(j) + off))
    row = lambda f: pl.BlockSpec((FFN_TF, D), lambda i, j: (f(j), 0))
    return pl.pallas_call(
        _ffn_kernel,
        out_shape=jax.ShapeDtypeStruct((T, D), F32),
        grid=(T // tm, steps),
        in_specs=[
            pl.BlockSpec((tm, D), lambda i, j: (i, 0), pipeline_mode=pl.Buffered(1)),
            col(first, 0), col(first, nj), row(first),
            col(second, 0), col(second, nj), row(second),
            vec, vec,
        ],
        out_specs=pl.BlockSpec((tm, D), lambda i, j: (i, 0)),
        scratch_shapes=[pltpu.VMEM((tm, D), BF16)],
        compiler_params=_params(("parallel", "arbitrary")),
        name="ffn",
    )(x, w_gu, w_gu, w_down, w_gu, w_gu, w_down, ln_g, ln_b)


PROJ_TM = 512


def _proj_kernel(x_ref, w_ref, o_ref, xb_ref):
    @pl.when(pl.program_id(1) == 0)
    def _():
        xb_ref[...] = x_ref[...].astype(BF16)

    o_ref[...] = jnp.dot(xb_ref[...], w_ref[...], preferred_element_type=F32)


def in_projection(x, w_in_r):
    T, D = x.shape
    tm = min(PROJ_TM, T)
    return pl.pallas_call(
        _proj_kernel,
        out_shape=jax.ShapeDtypeStruct((T, NP), F32),
        grid=(T // tm, NP // PROJ_TN),
        in_specs=[
            pl.BlockSpec((tm, D), lambda i, j: (i, 0)),
            pl.BlockSpec((D, PROJ_TN), lambda i, j: (0, j)),
        ],
        out_specs=pl.BlockSpec((tm, PROJ_TN), lambda i, j: (i, j)),
        scratch_shapes=[pltpu.VMEM((tm, D), BF16)],
        compiler_params=_params(("parallel", "arbitrary")),
        name="in_proj",
    )(x, w_in_r)


OUT_TM = 256


def _outproj_kernel(x_ref, y0_ref, y1_ref, y2_ref, y3_ref, w_ref, g_ref, b_ref, o_ref):
    for c in range(0, o_ref.shape[1], ACC_COLS):
        cols = slice(c, c + ACC_COLS)
        acc = DN_ALPHA * x_ref[:, cols]
        for idx, y_ref in enumerate((y0_ref, y1_ref, y2_ref, y3_ref)):
            acc = acc + jnp.dot(y_ref[...], w_ref[idx * GROUP_W:(idx + 1) * GROUP_W, cols],
                                preferred_element_type=F32)
        o_ref[:, cols] = acc
    _residual_layer_norm(o_ref, None, 1.0, g_ref, b_ref)


def out_projection(x, ys, w_out, ln_g, ln_b):
    T, D = x.shape
    tm = min(OUT_TM, T)
    vec = pl.BlockSpec((1, D), lambda i: (0, 0))
    yspec = pl.BlockSpec((tm, GROUP_W), lambda i: (i, 0))
    return pl.pallas_call(
        _outproj_kernel,
        out_shape=jax.ShapeDtypeStruct((T, D), F32),
        grid=(T // tm,),
        in_specs=[
            pl.BlockSpec((tm, D), lambda i: (i, 0)),
            yspec, yspec, yspec, yspec,
            pl.BlockSpec(w_out.shape, lambda i: (0, 0), pipeline_mode=pl.Buffered(1)),
            vec, vec,
        ],
        out_specs=pl.BlockSpec((tm, D), lambda i: (i, 0)),
        compiler_params=_params(("parallel",)),
        name="out_proj",
    )(x, *ys, w_out, ln_g, ln_b)


GLA_ROWS = 512
SUBLANES = 8
GLA_SUB = 16
GLA_GROUP = 4


def _cumsum_rows(x):
    n = x.shape[0]
    row = lax.broadcasted_iota(jnp.int32, x.shape, 0)
    sh = 1
    while sh < n:
        x = x + jnp.where(row >= sh, pltpu.roll(x, sh, axis=0), 0.0)
        sh *= 2
    return x


def _gla_kernel(q_ref, k_ref, v_ref, gg_ref, lr_ref, wg_ref, bg_ref, ng_ref, o_ref,
                st_ref, b_scr, k_scr, v_scr):
    C = GLA_CHUNK
    heads = range(GLA_GROUP)
    kcols = [slice(h * GLA_DK, (h + 1) * GLA_DK) for h in heads]
    vcols = [slice(h * GLA_DV, (h + 1) * GLA_DV) for h in heads]
    nt = (((1,), (1,)), ((), ()))

    @pl.when(pl.program_id(2) == 0)
    def _():
        st_ref[...] = jnp.zeros_like(st_ref)

    n_chunks = q_ref.shape[1] // C
    row8 = lax.broadcasted_iota(jnp.int32, (SUBLANES, 1), 0)

    def chunk(c, carry):
        r0 = pl.multiple_of(c * C, C)
        rows = pl.ds(r0, C)
        lr = lr_ref[0, rows, :]
        logits = [jnp.dot(lr, wg_ref[:, kcols[h]], preferred_element_type=F32,
                          precision=lax.Precision.HIGHEST) + bg_ref[:, kcols[h]] for h in heads]
        q = [q_ref[0, rows, kcols[h]] * (GLA_DK ** -0.5) for h in heads]
        k = [k_ref[0, rows, kcols[h]] for h in heads]
        v = [v_ref[0, rows, vcols[h]] for h in heads]
        g = [(jnp.minimum(x, 0.0) - jnp.log(1.0 + jnp.exp(-jnp.abs(x)))) / GLA_GATE_NORMALIZER for x in logits]
        b = [_cumsum_rows(x) for x in g]
        b_last = [x[C - 1:C, :] for x in b]
        for h in heads:
            b_scr[h] = b[h]
            k_scr[h] = k[h]
            v_scr[h] = v[h]
        st = [st_ref[h] for h in heads]
        o_inter = [lax.dot_general((q[h] * jnp.exp(b[h])).astype(BF16), st[h].astype(BF16), nt,
                                   preferred_element_type=F32) for h in heads]
        v_b = [x.astype(BF16) for x in v]
        for h in heads:
            kd = (k[h] * jnp.exp(b_last[h] - b[h])).astype(BF16)
            st_ref[h] = st[h] * jnp.exp(b_last[h]) + lax.dot_general(
                v_b[h], kd, (((0,), (0,)), ((), ())), preferred_element_type=F32)

        for s in range(0, C, GLA_SUB):
            blk_acc = [o_inter[h][s:s + GLA_SUB] for h in heads]
            if s > 0:
                att = []
                for h in heads:
                    b_s = b_scr[h, s:s + 1, :]
                    q_t = (q[h][s:s + GLA_SUB] * jnp.exp(b[h][s:s + GLA_SUB] - b_s)).astype(BF16)
                    k_t = jnp.concatenate([k[h][:s] * jnp.exp(b_s - b[h][:s]), jnp.zeros((C - s, GLA_DK), F32)],
                                          axis=0).astype(BF16)
                    att.append(lax.dot_general(q_t, k_t, nt, preferred_element_type=F32))
                blk_acc = [blk_acc[h] + jnp.dot(att[h].astype(BF16), v_b[h], preferred_element_type=F32)
                           for h in heads]
            for i0 in range(s, s + GLA_SUB, SUBLANES):
                for h in heads:
                    q_i = q[h][i0:i0 + SUBLANES]
                    b_i = b[h][i0:i0 + SUBLANES]
                    acc = blk_acc[h][i0 - s:i0 - s + SUBLANES]
                    for j in range(s, i0 + SUBLANES):
                        b_j = b_scr[h, j:j + 1, :]
                        k_j = k_scr[h, j:j + 1, :]
                        v_j = v_scr[h, j:j + 1, :]
                        e = jnp.exp(jnp.minimum(b_i - b_j, 0.0))
                        a = jnp.sum(q_i * k_j * e, axis=-1, keepdims=True)
                        if j >= i0:
                            a = jnp.where(row8 >= (j - i0), a, 0.0)
                        acc = acc + a * v_j
                    acc = acc * lax.rsqrt(jnp.mean(acc * acc, axis=-1, keepdims=True) + RMS_EPS) * ng_ref[...]
                    gg = gg_ref[0, pl.ds(r0 + i0, SUBLANES), vcols[h]]
                    o_ref[0, pl.ds(r0 + i0, SUBLANES), vcols[h]] = (
                        acc * (gg / (1.0 + jnp.exp(-gg)))).astype(o_ref.dtype)
        return carry

    lax.fori_loop(0, n_chunks, chunk, 0)


def gla_mixer(P, wg_pad, bg, ng):
    B, S, _ = P.shape
    R = min(GLA_ROWS, S)
    kw = GLA_GROUP * GLA_DK
    vw = GLA_GROUP * GLA_DV
    kb = lambda base: (lambda b, h, r: (b, r, base // kw + h))
    vb = lambda base: (lambda b, h, r: (b, r, base // vw + h))
    return pl.pallas_call(
        _gla_kernel,
        out_shape=jax.ShapeDtypeStruct((B, S, GROUP_W), BF16),
        grid=(B, GLA_HEADS // GLA_GROUP, S // R),
        in_specs=[
            pl.BlockSpec((1, R, kw), kb(COL_GQ)),
            pl.BlockSpec((1, R, kw), kb(COL_GK)),
            pl.BlockSpec((1, R, vw), vb(COL_GV)),
            pl.BlockSpec((1, R, vw), vb(COL_GG)),
            pl.BlockSpec((1, R, LANES), lambda b, h, r: (b, r, COL_LR // LANES)),
            pl.BlockSpec((LANES, kw), lambda b, h, r: (0, h)),
            pl.BlockSpec((1, kw), lambda b, h, r: (0, h)),
            pl.BlockSpec((1, GLA_DV), lambda b, h, r: (0, 0)),
        ],
        out_specs=pl.BlockSpec((1, R, vw), lambda b, h, r: (b, r, h)),
        scratch_shapes=[
            pltpu.VMEM((GLA_GROUP, GLA_DV, GLA_DK), F32),
            pltpu.VMEM((GLA_GROUP, GLA_CHUNK, GLA_DK), F32),
            pltpu.VMEM((GLA_GROUP, GLA_CHUNK, GLA_DK), F32),
            pltpu.VMEM((GLA_GROUP, GLA_CHUNK, GLA_DV), F32),
        ],
        compiler_params=_params(("parallel", "parallel", "arbitrary")),
        name="gla",
    )(P, P, P, P, P, wg_pad, bg, ng)


def _swa_kernel(sink_ref, q_ref, kc_ref, kp_ref, vc_ref, vp_ref, cc_ref, sc_ref, cp_ref, sp_ref, o_ref):
    n = pl.program_id(1)
    BLK = SWA_BLOCK
    nt = (((1,), (1,)), ((), ()))
    cos, sin = cc_ref[0], sc_ref[0]
    cos_p, sin_p = cp_ref[0], sp_ref[0]
    qpos = BLK + lax.broadcasted_iota(jnp.int32, (BLK, 2 * BLK), 0)
    kpos = lax.broadcasted_iota(jnp.int32, (BLK, 2 * BLK), 1)
    rel = qpos - kpos
    mask = (rel >= 0) & (rel < SWA_WINDOW) & ((kpos >= BLK) | (n > 0))
    kvs = range(SWA_KV_HEADS)
    hd = lambda h: slice(h * HEAD_DIM, (h + 1) * HEAD_DIM)
    k_cat = [jnp.concatenate([_rope(kp_ref[0, :, hd(kv)], cos_p, sin_p), _rope(kc_ref[0, :, hd(kv)], cos, sin)],
                             axis=0).astype(BF16) for kv in kvs]
    v_cat = [jnp.concatenate([vp_ref[0, :, hd(kv)], vc_ref[0, :, hd(kv)]], axis=0).astype(BF16) for kv in kvs]
    heads = range(SWA_HEADS)
    q = [_rope(q_ref[0, :, hd(h)], cos, sin).astype(BF16) for h in heads]
    s = [lax.dot_general(q[h], k_cat[h // SWA_GROUP], nt, preferred_element_type=F32) * (HEAD_DIM ** -0.5)
         for h in heads]
    s = [jnp.where(mask, x, NEG_INF) for x in s]
    sink = [sink_ref[h] for h in heads]
    m = [jnp.maximum(jnp.max(s[h], axis=-1, keepdims=True), sink[h]) for h in heads]
    p = [jnp.exp(s[h] - m[h]) for h in heads]
    den = [jnp.sum(p[h], axis=-1, keepdims=True) + jnp.exp(sink[h] - m[h]) for h in heads]
    o = [jnp.dot(p[h].astype(BF16), v_cat[h // SWA_GROUP], preferred_element_type=F32) / den[h] for h in heads]
    for h in heads:
        o_ref[0, :, hd(h)] = o[h].astype(o_ref.dtype)


def swa_mixer(P, cos, sin, sinks):
    B, S, _ = P.shape
    BLK = SWA_BLOCK
    kvw = SWA_KV_HEADS * HEAD_DIM
    cur = lambda base, w: (lambda b, n, s: (b, n, base // w))
    prev = lambda base, w: (lambda b, n, s: (b, jnp.maximum(n - 1, 0), base // w))
    kvb = lambda f: pl.BlockSpec((1, BLK, kvw), f)
    tab = lambda f: pl.BlockSpec((1, BLK, HEAD_DIM), f)
    return pl.pallas_call(
        _swa_kernel,
        out_shape=jax.ShapeDtypeStruct((B, S, GROUP_W), BF16),
        grid_spec=pltpu.PrefetchScalarGridSpec(
            num_scalar_prefetch=1,
            grid=(B, S // BLK),
            in_specs=[
                pl.BlockSpec((1, BLK, GROUP_W), cur(COL_SQ, GROUP_W)),
                kvb(cur(COL_SK, kvw)), kvb(prev(COL_SK, kvw)), kvb(cur(COL_SV, kvw)), kvb(prev(COL_SV, kvw)),
                tab(cur(0, HEAD_DIM)), tab(cur(0, HEAD_DIM)), tab(prev(0, HEAD_DIM)), tab(prev(0, HEAD_DIM)),
            ],
            out_specs=pl.BlockSpec((1, BLK, GROUP_W), lambda b, n, s: (b, n, 0)),
        ),
        compiler_params=_params(("parallel", "arbitrary")),
        name="swa",
    )(sinks, P, P, P, P, P, cos, sin, cos, sin)


def _moba_kernel(q_ref, k_ref, v_ref, cos_ref, sin_ref, o_ref, kr_ref, vb_ref, kbar_ref):
    i = pl.program_id(2)
    BLK = MOBA_BLOCK
    S = k_ref.shape[1]
    NB = S // BLK
    heads = range(MOBA_GROUP)
    cols = [slice(h * HEAD_DIM, (h + 1) * HEAD_DIM) for h in heads]
    nt = (((1,), (1,)), ((), ()))

    @pl.when(i == 0)
    def _():
        for h in heads:
            kr = _rope(k_ref[0, :, cols[h]], cos_ref[0], sin_ref[0])
            kr_ref[:, cols[h]] = kr.astype(BF16)
            vb_ref[:, cols[h]] = v_ref[0, :, cols[h]].astype(BF16)
            for n in range(NB):
                kbar_ref[h, n:n + 1, :] = jnp.mean(kr[n * BLK:(n + 1) * BLK], axis=0, keepdims=True)

    rows = pl.ds(pl.multiple_of(i * BLK, BLK), BLK)
    cos_q, sin_q = cos_ref[0, rows, :], sin_ref[0, rows, :]
    q = [_rope(q_ref[0, :, cols[h]], cos_q, sin_q) for h in heads]

    gate = [lax.dot_general(kbar_ref[h], q[h], nt, preferred_element_type=F32,
                            precision=lax.Precision.HIGHEST) for h in heads]
    blk = lax.broadcasted_iota(jnp.int32, (NB, BLK), 0)
    past = blk < i
    sel = []
    for h in heads:
        sel_h = []
        for n in range(NB):
            g_n = gate[h][n:n + 1, :]
            ahead = ((gate[h] > g_n) | ((gate[h] == g_n) & (blk < n))) & past
            rank = jnp.sum(jnp.where(ahead, 1.0, 0.0), axis=0, keepdims=True)
            sel_h.append(rank < float(MOBA_TOPK))
        sel.append(sel_h)

    qb = [x.astype(BF16) for x in q]
    key = lax.broadcasted_iota(jnp.int32, (BLK, BLK), 0)
    qry = lax.broadcasted_iota(jnp.int32, (BLK, BLK), 1)
    causal = key <= qry

    def attend(nb):
        blocks = range(nb)
        scores = [[lax.dot_general(kr_ref[n * BLK:(n + 1) * BLK, cols[h]], qb[h], nt,
                                   preferred_element_type=F32) for n in blocks] for h in heads]
        masked = [[jnp.where((sel[h][n] & (n < i)) | (causal & (n == i)),
                             scores[h][n] * (HEAD_DIM ** -0.5), NEG_INF) for n in blocks] for h in heads]
        m = [functools.reduce(jnp.maximum, [jnp.max(x, axis=0, keepdims=True) for x in masked[h]]) for h in heads]
        probs = [[jnp.exp(x - m[h]) for x in masked[h]] for h in heads]
        den = [functools.reduce(jnp.add, [jnp.sum(p, axis=0, keepdims=True) for p in probs[h]]) for h in heads]
        outs = [[lax.dot_general(probs[h][n].astype(BF16), vb_ref[n * BLK:(n + 1) * BLK, cols[h]],
                                 (((0,), (0,)), ((), ())), preferred_element_type=F32) for n in blocks]
                for h in heads]
        for h in heads:
            inv = jnp.transpose(jnp.broadcast_to(1.0 / den[h], (HEAD_DIM, BLK)))
            o_ref[0, :, cols[h]] = (functools.reduce(jnp.add, outs[h]) * inv).astype(o_ref.dtype)

    for nb in range(2, NB + 1, 2):
        @pl.when((i >= nb - 2) & (i < nb))
        def _(nb=nb):
            attend(nb)


def moba_mixer(P, cos, sin):
    B, S, _ = P.shape
    BLK = MOBA_BLOCK
    W = MOBA_GROUP * HEAD_DIM
    full = lambda base: pl.BlockSpec((1, S, W), lambda b, h, i: (b, 0, base // W + h))
    tab = pl.BlockSpec((1, S, HEAD_DIM), lambda b, h, i: (b, 0, 0))
    return pl.pallas_call(
        _moba_kernel,
        out_shape=jax.ShapeDtypeStruct((B, S, GROUP_W), BF16),
        grid=(B, MOBA_HEADS // MOBA_GROUP, S // BLK),
        in_specs=[
            pl.BlockSpec((1, BLK, W), lambda b, h, i: (b, i, COL_MQ // W + h)),
            full(COL_MK), full(COL_MV), tab, tab,
        ],
        out_specs=pl.BlockSpec((1, BLK, W), lambda b, h, i: (b, i, h)),
        scratch_shapes=[
            pltpu.VMEM((S, W), BF16),
            pltpu.VMEM((S, W), BF16),
            pltpu.VMEM((MOBA_GROUP, S // BLK, HEAD_DIM), F32),
        ],
        compiler_params=_params(("parallel", "parallel", "arbitrary")),
        name="moba",
    )(P, P, P, cos, sin)


SB_TILE = 256
SB_GROUP = 4


def _sb_kernel(q_ref, k_ref, v_ref, o_ref, qb_ref, run_ref, acc_ref):
    i = pl.program_id(2)
    T = SB_TILE
    qb_ref[...] = (q_ref[0] * (HEAD_DIM ** -0.5)).astype(BF16)
    srow = lax.broadcasted_iota(jnp.int32, (T, T), 0)
    scol = lax.broadcasted_iota(jnp.int32, (T, T), 1)
    upper = jnp.where(srow > scol, 1.0, 0.0).astype(BF16)
    strict = scol < srow

    def tile(j, diagonal):
        rows = pl.ds(pl.multiple_of(j * T, T), T)
        heads = range(SB_GROUP)
        cols = [slice(h * HEAD_DIM, (h + 1) * HEAD_DIM) for h in heads]
        z = [lax.dot_general(qb_ref[:, cols[h]], k_ref[0, rows, cols[h]].astype(BF16), (((1,), (1,)), ((), ())),
                             preferred_element_type=F32) for h in heads]
        soft = [jnp.log(1.0 + jnp.exp(-jnp.abs(z[h]))) for h in heads]
        log_keep = [-jnp.maximum(z[h], 0.0) - soft[h] for h in heads]
        log_beta = [log_keep[h] + z[h] for h in heads]
        if diagonal:
            log_keep = [jnp.where(strict, log_keep[h], 0.0) for h in heads]
        suffix = [jnp.dot(log_keep[h].astype(BF16), upper, preferred_element_type=F32) for h in heads]
        total = [suffix[h][:, 0:1] + log_keep[h][:, 0:1] for h in heads]
        if diagonal:
            w = [jnp.where(strict, jnp.exp(log_beta[h] + suffix[h]), 0.0) for h in heads]
        else:
            w = [jnp.exp(log_beta[h] + (suffix[h] + run_ref[h])) for h in heads]
        pv = [jnp.dot(w[h].astype(BF16), v_ref[0, rows, cols[h]].astype(BF16), preferred_element_type=F32)
              for h in heads]
        for h in heads:
            if diagonal:
                acc_ref[h] = pv[h]
                run_ref[h] = total[h]
            else:
                acc_ref[h] += pv[h]
                run_ref[h] += total[h]

    tile(i, True)

    def body(t, carry):
        tile(i - 1 - t, False)
        return carry

    lax.fori_loop(0, i, body, 0)
    for h in range(SB_GROUP):
        o_ref[0, :, h * HEAD_DIM:(h + 1) * HEAD_DIM] = acc_ref[h].astype(o_ref.dtype)


def sb_mixer(P):
    B, S, _ = P.shape
    T = SB_TILE
    W = SB_GROUP * HEAD_DIM
    full = lambda base: pl.BlockSpec((1, S, W), lambda b, h, i: (b, 0, base // W + h))
    return pl.pallas_call(
        _sb_kernel,
        out_shape=jax.ShapeDtypeStruct((B, S, GROUP_W), BF16),
        grid=(B, SB_HEADS // SB_GROUP, S // T),
        in_specs=[
            pl.BlockSpec((1, T, W), lambda b, h, i: (b, i, COL_BQ // W + h)),
            full(COL_BK), full(COL_BV),
        ],
        out_specs=pl.BlockSpec((1, T, W), lambda b, h, i: (b, i, h)),
        scratch_shapes=[
            pltpu.VMEM((T, W), BF16),
            pltpu.VMEM((SB_GROUP, T, 1), F32),
            pltpu.VMEM((SB_GROUP, T, HEAD_DIM), F32),
        ],
        compiler_params=_params(("parallel", "parallel", "arbitrary")),
        name="stick_breaking",
    )(P, P, P)


CAST_ROWS = 256
CAST_BLOCK_BYTES = 6 * 1024 * 1024


def _cast_kernel(w_ref, o_ref):
    o_ref[...] = w_ref[...].astype(o_ref.dtype)


def cast_bf16(w, layer):
    _, R, C = w.shape
    tr = min(CAST_ROWS, R)
    tc = next(C // k for k in range(1, C // LANES + 1)
              if C % k == 0 and (C // k) % LANES == 0 and tr * (C // k) * 4 <= CAST_BLOCK_BYTES)
    return pl.pallas_call(
        _cast_kernel,
        out_shape=jax.ShapeDtypeStruct((R, C), BF16),
        grid=(R // tr, C // tc),
        in_specs=[pl.BlockSpec((None, tr, tc), lambda i, j: (layer, i, j))],
        out_specs=pl.BlockSpec((tr, tc), lambda i, j: (i, j)),
        compiler_params=_params(("parallel", "parallel")),
        name="cast_bf16",
    )(w)


RELAYOUT_COLS = 256


def _relayout_kernel(a_ref, b_ref, o_ref):
    ob = pl.program_id(0)
    lr_blk = (COL_GG + GROUP_W) // RELAYOUT_COLS
    end_blk = COL_LR // RELAYOUT_COLS
    rank = GLA_GATE_RANK

    @pl.when(ob < lr_blk)
    def _():
        o_ref[...] = jnp.transpose(a_ref[...]).astype(BF16)

    @pl.when((ob >= lr_blk) & (ob < end_blk))
    def _():
        w = jnp.concatenate([a_ref[rank:, :], b_ref[:rank, :]], axis=0)
        o_ref[...] = jnp.transpose(w).astype(BF16)

    @pl.when(ob == end_blk)
    def _():
        w = jnp.concatenate([a_ref[:rank, :], jnp.zeros((RELAYOUT_COLS - rank, a_ref.shape[1]), F32)], axis=0)
        o_ref[...] = jnp.transpose(w).astype(BF16)

    @pl.when(ob > end_blk)
    def _():
        o_ref[...] = jnp.zeros(o_ref.shape, BF16)


def relayout_w_in(w, layer):
    _, R, C = w.shape
    wt = jnp.swapaxes(w, 1, 2)
    lr_blk = (COL_GG + GROUP_W) // RELAYOUT_COLS
    end_blk = COL_LR // RELAYOUT_COLS
    blk = lambda f: pl.BlockSpec((None, RELAYOUT_COLS, R), f)
    return pl.pallas_call(
        _relayout_kernel,
        out_shape=jax.ShapeDtypeStruct((R, NP), BF16),
        grid=(NP // RELAYOUT_COLS,),
        in_specs=[
            blk(lambda ob: (layer, jnp.where(ob == end_blk, lr_blk, jnp.minimum(ob, end_blk)), 0)),
            blk(lambda ob: (layer, jnp.minimum(ob + 1, end_blk), 0)),
        ],
        out_specs=pl.BlockSpec((R, RELAYOUT_COLS), lambda ob: (0, ob)),
        compiler_params=_params(("parallel",)),
        name="relayout_w_in",
    )(wt, wt)


def token_mixing(x2, B, S, cos, sin, w_in_r, wg_pad, bg, ng, sinks, w_out_b, ln_g, ln_b):
    P = in_projection(x2, w_in_r).reshape(B, S, NP)
    ys = (gla_mixer(P, wg_pad, bg, ng), swa_mixer(P, cos, sin, sinks), moba_mixer(P, cos, sin), sb_mixer(P))
    ys = [y.reshape(B * S, GROUP_W) for y in ys]
    return out_projection(x2, ys, w_out_b, ln_g, ln_b)


def kernel(x, positions, w_in, gla_w_gate_up, gla_b_gate_up, gla_norm_g, swa_sinks, w_out,
           ffn1_w_gu, ffn1_w_down, ffn2_w_gu, ffn2_w_down, ln_g, ln_b):
    B, S, D = x.shape
    cos, sin = rope_tables(positions)
    x2 = x.reshape(B * S, D)
    for l in range(DEPTH):
        g = ln_g[l].reshape(3, 1, D)
        b = ln_b[l].reshape(3, 1, D)
        x2 = ffn_sublayer(x2, cast_bf16(ffn1_w_gu, l), cast_bf16(ffn1_w_down, l), g[0], b[0])
        wg_pad = jnp.zeros((LANES, GLA_KEY), F32).at[:GLA_GATE_RANK].set(gla_w_gate_up[l])
        x2 = token_mixing(x2, B, S, cos, sin, relayout_w_in(w_in, l), wg_pad,
                          gla_b_gate_up[l].reshape(1, GLA_KEY), gla_norm_g[l].reshape(1, GLA_DV),
                          swa_sinks[l], cast_bf16(w_out, l), g[1], b[1])
        x2 = ffn_sublayer(x2, cast_bf16(ffn2_w_gu, l), cast_bf16(ffn2_w_down, l), g[2], b[2])
    return x2.reshape(B, S, D)
```

```python
import functools

import jax
import jax.numpy as jnp
from jax import lax
from jax.experimental import pallas as pl
from jax.experimental.pallas import tpu as pltpu

F32 = jnp.float32
BF16 = jnp.bfloat16

D_MODEL = 4096
DEPTH = 2
GROUP_W = 1024
HEAD_DIM = 128
GLA_HEADS = 4
GLA_DV = 256
GLA_DK = 128
GLA_KEY = 512
GLA_GATE_RANK = 16
GLA_GATE_NORMALIZER = 16.0
GLA_CHUNK = 64
SWA_HEADS = 8
SWA_KV_HEADS = 2
SWA_GROUP = SWA_HEADS // SWA_KV_HEADS
SWA_WINDOW = 128
SWA_BLOCK = 128
MOBA_HEADS = 8
MOBA_BLOCK = 256
MOBA_TOPK = 3
MOBA_GROUP = 2
SB_HEADS = 8
ROPE_THETA = 10000.0
D_FF = 11008
FFN_RES = 0.5
LN_EPS = 1e-5
RMS_EPS = 1e-5
DN_ALPHA = (2 * DEPTH) ** 0.25

LANES = 128
VMEM_LIMIT = 56 * 1024 * 1024

COL_GQ, COL_GK, COL_GV, COL_GG = 0, 512, 1024, 2048
COL_SQ, COL_SK, COL_SV = 3072, 4096, 4352
COL_MQ, COL_MK, COL_MV = 4608, 5632, 6656
COL_BQ, COL_BK, COL_BV = 7680, 8704, 9728
COL_LR = 10752
PROJ_TN = 1024
NP = 11264

NEG_INF = float("-inf")


def _params(sem, vmem=VMEM_LIMIT):
    return pltpu.CompilerParams(dimension_semantics=sem, vmem_limit_bytes=vmem)


def _layer_norm_rows(y, g, b):
    mu = jnp.mean(y, axis=-1, keepdims=True)
    yc = y - mu
    var = jnp.mean(yc * yc, axis=-1, keepdims=True)
    return yc * lax.rsqrt(var + LN_EPS) * g + b


LN_ROWS = 32
ACC_COLS = 512


def _residual_layer_norm(o_ref, x_ref, res_scale, g_ref, b_ref):
    def body(r, carry):
        rows = pl.ds(pl.multiple_of(r * LN_ROWS, LN_ROWS), LN_ROWS)
        y = o_ref[rows, :]
        if x_ref is not None:
            y = DN_ALPHA * x_ref[rows, :] + res_scale * y
        o_ref[rows, :] = _layer_norm_rows(y, g_ref[...], b_ref[...])
        return carry

    lax.fori_loop(0, o_ref.shape[0] // LN_ROWS, body, 0, unroll=2)


def _accumulate_dot(o_ref, a, w_ref):
    for c in range(0, o_ref.shape[1], ACC_COLS):
        o_ref[:, c:c + ACC_COLS] += jnp.dot(a, w_ref[:, c:c + ACC_COLS], preferred_element_type=F32)


def _rope_kernel(pos_ref, inv_ref, sign_ref, cos_ref, sin_ref):
    ang = pos_ref[0].astype(F32) * inv_ref[...]
    cos_ref[0] = jnp.cos(ang)
    sin_ref[0] = jnp.sin(ang) * sign_ref[...]


def rope_tables(positions):
    B, S = positions.shape
    ts = min(S, 512)
    inv = 1.0 / (ROPE_THETA ** (jnp.arange(0, HEAD_DIM, 2, dtype=F32) / HEAD_DIM))
    inv_full = jnp.concatenate([inv, inv]).reshape(1, HEAD_DIM)
    sign = jnp.concatenate([-jnp.ones((HEAD_DIM // 2,), F32), jnp.ones((HEAD_DIM // 2,), F32)]).reshape(1, HEAD_DIM)
    pos_b = jnp.broadcast_to(positions[:, :, None], (B, S, HEAD_DIM))
    blk = pl.BlockSpec((1, ts, HEAD_DIM), lambda b, s: (b, s, 0))
    vec = pl.BlockSpec((1, HEAD_DIM), lambda b, s: (0, 0))
    return pl.pallas_call(
        _rope_kernel,
        out_shape=(jax.ShapeDtypeStruct((B, S, HEAD_DIM), F32),) * 2,
        grid=(B, S // ts),
        in_specs=[blk, vec, vec],
        out_specs=(blk, blk),
        compiler_params=_params(("parallel", "parallel")),
        name="rope_tables",
    )(pos_b, inv_full, sign)


def _rope(x, cos, sin):
    return x * cos + pltpu.roll(x, HEAD_DIM // 2, axis=1) * sin


FFN_TM = 512
FFN_TF = 256


def _ffn_kernel(x_ref, wg_ref, wu_ref, wd_ref, g_ref, b_ref, o_ref, xb_ref, act_ref):
    j = pl.program_id(1)
    last = pl.num_programs(1) - 1

    def activation():
        xb = xb_ref[...]
        gate = jnp.dot(xb, wg_ref[...], preferred_element_type=F32)
        up = jnp.dot(xb, wu_ref[...], preferred_element_type=F32)
        return (gate / (1.0 + jnp.exp(-gate)) * up).astype(BF16)

    @pl.when(j == 0)
    def _():
        xb_ref[...] = x_ref[...].astype(BF16)
        o_ref[...] = jnp.zeros_like(o_ref)
        act_ref[...] = activation()

    @pl.when((j > 0) & (j < last))
    def _():
        prev = act_ref[...]
        act = activation()
        _accumulate_dot(o_ref, prev, wd_ref)
        act_ref[...] = act

    @pl.when(j == last)
    def _():
        _accumulate_dot(o_ref, act_ref[...], wd_ref)
        _residual_layer_norm(o_ref, x_ref, FFN_RES, g_ref, b_ref)


def ffn_sublayer(x, w_gu, w_down, ln_g, ln_b):
    T, D = x.shape
    tm = min(FFN_TM, T)
    nj = D_FF // FFN_TF
    up_chunk = lambda j: jnp.minimum(j, nj - 1)
    down_chunk = lambda j: jnp.maximum(j - 1, 0)
    vec = pl.BlockSpec((1, D), lambda i, j: (0, 0))
    return pl.pallas_call(
        _ffn_kernel,
        out_shape=jax.ShapeDtypeStruct((T, D), F32),
        grid=(T // tm, nj + 1),
        in_specs=[
            pl.BlockSpec((tm, D), lambda i, j: (i, 0)),
            pl.BlockSpec((D, FFN_TF), lambda i, j: (0, up_chunk(j))),
            pl.BlockSpec((D, FFN_TF), lambda i, j: (0, up_chunk(j) + nj)),
            pl.BlockSpec((FFN_TF, D), lambda i, j: (down_chunk(j), 0)),
            vec, vec,
        ],
        out_specs=pl.BlockSpec((tm, D), lambda i, j: (i, 0)),
        scratch_shapes=[pltpu.VMEM((tm, D), BF16), pltpu.VMEM((tm, FFN_TF), BF16)],
        compiler_params=_params(("parallel", "arbitrary")),
        name="ffn",
    )(x, w_gu, w_gu, w_down, ln_g, ln_b)


PROJ_TM = 512


def _proj_kernel(x_ref, w_ref, o_ref, xb_ref):
    @pl.when(pl.program_id(1) == 0)
    def _():
        xb_ref[...] = x_ref[...].astype(BF16)

    o_ref[...] = jnp.dot(xb_ref[...], w_ref[...], preferred_element_type=F32)


def in_projection(x, w_in_r):
    T, D = x.shape
    tm = min(PROJ_TM, T)
    return pl.pallas_call(
        _proj_kernel,
        out_shape=jax.ShapeDtypeStruct((T, NP), F32),
        grid=(T // tm, NP // PROJ_TN),
        in_specs=[
            pl.BlockSpec((tm, D), lambda i, j: (i, 0)),
            pl.BlockSpec((D, PROJ_TN), lambda i, j: (0, j)),
        ],
        out_specs=pl.BlockSpec((tm, PROJ_TN), lambda i, j: (i, j)),
        scratch_shapes=[pltpu.VMEM((tm, D), BF16)],
        compiler_params=_params(("parallel", "arbitrary")),
        name="in_proj",
    )(x, w_in_r)


OUT_TM = 256


def _outproj_kernel(x_ref, y0_ref, y1_ref, y2_ref, y3_ref, w_ref, g_ref, b_ref, o_ref):
    for c in range(0, o_ref.shape[1], ACC_COLS):
        cols = slice(c, c + ACC_COLS)
        acc = DN_ALPHA * x_ref[:, cols]
        for idx, y_ref in enumerate((y0_ref, y1_ref, y2_ref, y3_ref)):
            acc = acc + jnp.dot(y_ref[...], w_ref[idx * GROUP_W:(idx + 1) * GROUP_W, cols],
                                preferred_element_type=F32)
        o_ref[:, cols] = acc
    _residual_layer_norm(o_ref, None, 1.0, g_ref, b_ref)


def out_projection(x, ys, w_out, ln_g, ln_b):
    T, D = x.shape
    tm = min(OUT_TM, T)
    vec = pl.BlockSpec((1, D), lambda i: (0, 0))
    yspec = pl.BlockSpec((tm, GROUP_W), lambda i: (i, 0))
    return pl.pallas_call(
        _outproj_kernel,
        out_shape=jax.ShapeDtypeStruct((T, D), F32),
        grid=(T // tm,),
        in_specs=[
            pl.BlockSpec((tm, D), lambda i: (i, 0)),
            yspec, yspec, yspec, yspec,
            pl.BlockSpec(w_out.shape, lambda i: (0, 0), pipeline_mode=pl.Buffered(1)),
            vec, vec,
        ],
        out_specs=pl.BlockSpec((tm, D), lambda i: (i, 0)),
        compiler_params=_params(("parallel",)),
        name="out_proj",
    )(x, *ys, w_out, ln_g, ln_b)


GLA_ROWS = 512
SUBLANES = 8
GLA_SUB = 16
GLA_GROUP = 4


def _cumsum_rows(x):
    n = x.shape[0]
    row = lax.broadcasted_iota(jnp.int32, x.shape, 0)
    sh = 1
    while sh < n:
        x = x + jnp.where(row >= sh, pltpu.roll(x, sh, axis=0), 0.0)
        sh *= 2
    return x


def _gla_kernel(q_ref, k_ref, v_ref, gg_ref, lr_ref, wg_ref, bg_ref, ng_ref, o_ref,
                st_ref, b_scr, k_scr, v_scr):
    C = GLA_CHUNK
    heads = range(GLA_GROUP)
    kcols = [slice(h * GLA_DK, (h + 1) * GLA_DK) for h in heads]
    vcols = [slice(h * GLA_DV, (h + 1) * GLA_DV) for h in heads]
    nt = (((1,), (1,)), ((), ()))

    @pl.when(pl.program_id(2) == 0)
    def _():
        st_ref[...] = jnp.zeros_like(st_ref)

    n_chunks = q_ref.shape[1] // C
    row8 = lax.broadcasted_iota(jnp.int32, (SUBLANES, 1), 0)

    def chunk(c, carry):
        r0 = pl.multiple_of(c * C, C)
        rows = pl.ds(r0, C)
        lr = lr_ref[0, rows, :]
        logits = [jnp.dot(lr, wg_ref[:, kcols[h]], preferred_element_type=F32,
                          precision=lax.Precision.HIGHEST) + bg_ref[:, kcols[h]] for h in heads]
        q = [q_ref[0, rows, kcols[h]] * (GLA_DK ** -0.5) for h in heads]
        k = [k_ref[0, rows, kcols[h]] for h in heads]
        v = [v_ref[0, rows, vcols[h]] for h in heads]
        g = [(jnp.minimum(x, 0.0) - jnp.log(1.0 + jnp.exp(-jnp.abs(x)))) / GLA_GATE_NORMALIZER for x in logits]
        b = [_cumsum_rows(x) for x in g]
        b_last = [x[C - 1:C, :] for x in b]
        for h in heads:
            b_scr[h] = b[h]
            k_scr[h] = k[h]
            v_scr[h] = v[h]
        st = [st_ref[h] for h in heads]
        o_inter = [lax.dot_general((q[h] * jnp.exp(b[h])).astype(BF16), st[h].astype(BF16), nt,
                                   preferred_element_type=F32) for h in heads]
        v_b = [x.astype(BF16) for x in v]
        for h in heads:
            kd = (k[h] * jnp.exp(b_last[h] - b[h])).astype(BF16)
            st_ref[h] = st[h] * jnp.exp(b_last[h]) + lax.dot_general(
                v_b[h], kd, (((0,), (0,)), ((), ())), preferred_element_type=F32)

        for s in range(0, C, GLA_SUB):
            blk_acc = [o_inter[h][s:s + GLA_SUB] for h in heads]
            if s > 0:
                att = []
                for h in heads:
                    b_s = b_scr[h, s:s + 1, :]
                    q_t = (q[h][s:s + GLA_SUB] * jnp.exp(b[h][s:s + GLA_SUB] - b_s)).astype(BF16)
                    k_t = jnp.concatenate([k[h][:s] * jnp.exp(b_s - b[h][:s]), jnp.zeros((C - s, GLA_DK), F32)],
                                          axis=0).astype(BF16)
                    att.append(lax.dot_general(q_t, k_t, nt, preferred_element_type=F32))
                blk_acc = [blk_acc[h] + jnp.dot(att[h].astype(BF16), v_b[h], preferred_element_type=F32)
                           for h in heads]
            for i0 in range(s, s + GLA_SUB, SUBLANES):
                for h in heads:
                    q_i = q[h][i0:i0 + SUBLANES]
                    b_i = b[h][i0:i0 + SUBLANES]
                    acc = blk_acc[h][i0 - s:i0 - s + SUBLANES]
                    for j in range(s, i0 + SUBLANES):
                        b_j = b_scr[h, j:j + 1, :]
                        k_j = k_scr[h, j:j + 1, :]
                        v_j = v_scr[h, j:j + 1, :]
                        e = jnp.exp(jnp.minimum(b_i - b_j, 0.0))
                        a = jnp.sum(q_i * k_j * e, axis=-1, keepdims=True)
                        if j >= i0:
                            a = jnp.where(row8 >= (j - i0), a, 0.0)
                        acc = acc + a * v_j
                    acc = acc * lax.rsqrt(jnp.mean(acc * acc, axis=-1, keepdims=True) + RMS_EPS) * ng_ref[...]
                    gg = gg_ref[0, pl.ds(r0 + i0, SUBLANES), vcols[h]]
                    o_ref[0, pl.ds(r0 + i0, SUBLANES), vcols[h]] = (
                        acc * (gg / (1.0 + jnp.exp(-gg)))).astype(o_ref.dtype)
        return carry

    lax.fori_loop(0, n_chunks, chunk, 0)


def gla_mixer(P, wg_pad, bg, ng):
    B, S, _ = P.shape
    R = min(GLA_ROWS, S)
    kw = GLA_GROUP * GLA_DK
    vw = GLA_GROUP * GLA_DV
    kb = lambda base: (lambda b, h, r: (b, r, base // kw + h))
    vb = lambda base: (lambda b, h, r: (b, r, base // vw + h))
    return pl.pallas_call(
        _gla_kernel,
        out_shape=jax.ShapeDtypeStruct((B, S, GROUP_W), BF16),
        grid=(B, GLA_HEADS // GLA_GROUP, S // R),
        in_specs=[
            pl.BlockSpec((1, R, kw), kb(COL_GQ)),
            pl.BlockSpec((1, R, kw), kb(COL_GK)),
            pl.BlockSpec((1, R, vw), vb(COL_GV)),
            pl.BlockSpec((1, R, vw), vb(COL_GG)),
            pl.BlockSpec((1, R, LANES), lambda b, h, r: (b, r, COL_LR // LANES)),
            pl.BlockSpec((LANES, kw), lambda b, h, r: (0, h)),
            pl.BlockSpec((1, kw), lambda b, h, r: (0, h)),
            pl.BlockSpec((1, GLA_DV), lambda b, h, r: (0, 0)),
        ],
        out_specs=pl.BlockSpec((1, R, vw), lambda b, h, r: (b, r, h)),
        scratch_shapes=[
            pltpu.VMEM((GLA_GROUP, GLA_DV, GLA_DK), F32),
            pltpu.VMEM((GLA_GROUP, GLA_CHUNK, GLA_DK), F32),
            pltpu.VMEM((GLA_GROUP, GLA_CHUNK, GLA_DK), F32),
            pltpu.VMEM((GLA_GROUP, GLA_CHUNK, GLA_DV), F32),
        ],
        compiler_params=_params(("parallel", "parallel", "arbitrary")),
        name="gla",
    )(P, P, P, P, P, wg_pad, bg, ng)


def _swa_kernel(sink_ref, q_ref, kc_ref, kp_ref, vc_ref, vp_ref, cc_ref, sc_ref, cp_ref, sp_ref, o_ref):
    n = pl.program_id(1)
    BLK = SWA_BLOCK
    nt = (((1,), (1,)), ((), ()))
    cos, sin = cc_ref[0], sc_ref[0]
    cos_p, sin_p = cp_ref[0], sp_ref[0]
    qpos = BLK + lax.broadcasted_iota(jnp.int32, (BLK, 2 * BLK), 0)
    kpos = lax.broadcasted_iota(jnp.int32, (BLK, 2 * BLK), 1)
    rel = qpos - kpos
    mask = (rel >= 0) & (rel < SWA_WINDOW) & ((kpos >= BLK) | (n > 0))
    kvs = range(SWA_KV_HEADS)
    hd = lambda h: slice(h * HEAD_DIM, (h + 1) * HEAD_DIM)
    k_cat = [jnp.concatenate([_rope(kp_ref[0, :, hd(kv)], cos_p, sin_p), _rope(kc_ref[0, :, hd(kv)], cos, sin)],
                             axis=0).astype(BF16) for kv in kvs]
    v_cat = [jnp.concatenate([vp_ref[0, :, hd(kv)], vc_ref[0, :, hd(kv)]], axis=0).astype(BF16) for kv in kvs]
    heads = range(SWA_HEADS)
    q = [_rope(q_ref[0, :, hd(h)], cos, sin).astype(BF16) for h in heads]
    s = [lax.dot_general(q[h], k_cat[h // SWA_GROUP], nt, preferred_element_type=F32) * (HEAD_DIM ** -0.5)
         for h in heads]
    s = [jnp.where(mask, x, NEG_INF) for x in s]
    sink = [sink_ref[h] for h in heads]
    m = [jnp.maximum(jnp.max(s[h], axis=-1, keepdims=True), sink[h]) for h in heads]
    p = [jnp.exp(s[h] - m[h]) for h in heads]
    den = [jnp.sum(p[h], axis=-1, keepdims=True) + jnp.exp(sink[h] - m[h]) for h in heads]
    o = [jnp.dot(p[h].astype(BF16), v_cat[h // SWA_GROUP], preferred_element_type=F32) / den[h] for h in heads]
    for h in heads:
        o_ref[0, :, hd(h)] = o[h].astype(o_ref.dtype)


def swa_mixer(P, cos, sin, sinks):
    B, S, _ = P.shape
    BLK = SWA_BLOCK
    kvw = SWA_KV_HEADS * HEAD_DIM
    cur = lambda base, w: (lambda b, n, s: (b, n, base // w))
    prev = lambda base, w: (lambda b, n, s: (b, jnp.maximum(n - 1, 0), base // w))
    kvb = lambda f: pl.BlockSpec((1, BLK, kvw), f)
    tab = lambda f: pl.BlockSpec((1, BLK, HEAD_DIM), f)
    return pl.pallas_call(
        _swa_kernel,
        out_shape=jax.ShapeDtypeStruct((B, S, GROUP_W), BF16),
        grid_spec=pltpu.PrefetchScalarGridSpec(
            num_scalar_prefetch=1,
            grid=(B, S // BLK),
            in_specs=[
                pl.BlockSpec((1, BLK, GROUP_W), cur(COL_SQ, GROUP_W)),
                kvb(cur(COL_SK, kvw)), kvb(prev(COL_SK, kvw)), kvb(cur(COL_SV, kvw)), kvb(prev(COL_SV, kvw)),
                tab(cur(0, HEAD_DIM)), tab(cur(0, HEAD_DIM)), tab(prev(0, HEAD_DIM)), tab(prev(0, HEAD_DIM)),
            ],
            out_specs=pl.BlockSpec((1, BLK, GROUP_W), lambda b, n, s: (b, n, 0)),
        ),
        compiler_params=_params(("parallel", "arbitrary")),
        name="swa",
    )(sinks, P, P, P, P, P, cos, sin, cos, sin)


def _moba_kernel(q_ref, k_ref, v_ref, cos_ref, sin_ref, o_ref, kr_ref, vb_ref, kbar_ref):
    i = pl.program_id(2)
    BLK = MOBA_BLOCK
    S = k_ref.shape[1]
    NB = S // BLK
    heads = range(MOBA_GROUP)
    cols = [slice(h * HEAD_DIM, (h + 1) * HEAD_DIM) for h in heads]
    nt = (((1,), (1,)), ((), ()))

    @pl.when(i == 0)
    def _():
        for h in heads:
            kr = _rope(k_ref[0, :, cols[h]], cos_ref[0], sin_ref[0])
            kr_ref[:, cols[h]] = kr.astype(BF16)
            vb_ref[:, cols[h]] = v_ref[0, :, cols[h]].astype(BF16)
            for n in range(NB):
                kbar_ref[h, n:n + 1, :] = jnp.mean(kr[n * BLK:(n + 1) * BLK], axis=0, keepdims=True)

    rows = pl.ds(pl.multiple_of(i * BLK, BLK), BLK)
    cos_q, sin_q = cos_ref[0, rows, :], sin_ref[0, rows, :]
    q = [_rope(q_ref[0, :, cols[h]], cos_q, sin_q) for h in heads]

    gate = [lax.dot_general(kbar_ref[h], q[h], nt, preferred_element_type=F32,
                            precision=lax.Precision.HIGHEST) for h in heads]
    blk = lax.broadcasted_iota(jnp.int32, (NB, BLK), 0)
    past = blk < i
    sel = []
    for h in heads:
        sel_h = []
        for n in range(NB):
            g_n = gate[h][n:n + 1, :]
            ahead = ((gate[h] > g_n) | ((gate[h] == g_n) & (blk < n))) & past
            rank = jnp.sum(jnp.where(ahead, 1.0, 0.0), axis=0, keepdims=True)
            sel_h.append(rank < float(MOBA_TOPK))
        sel.append(sel_h)

    qb = [x.astype(BF16) for x in q]
    key = lax.broadcasted_iota(jnp.int32, (BLK, BLK), 0)
    qry = lax.broadcasted_iota(jnp.int32, (BLK, BLK), 1)
    causal = key <= qry

    def attend(nb):
        blocks = range(nb)
        scores = [[lax.dot_general(kr_ref[n * BLK:(n + 1) * BLK, cols[h]], qb[h], nt,
                                   preferred_element_type=F32) for n in blocks] for h in heads]
        masked = [[jnp.where((sel[h][n] & (n < i)) | (causal & (n == i)),
                             scores[h][n] * (HEAD_DIM ** -0.5), NEG_INF) for n in blocks] for h in heads]
        m = [functools.reduce(jnp.maximum, [jnp.max(x, axis=0, keepdims=True) for x in masked[h]]) for h in heads]
        probs = [[jnp.exp(x - m[h]) for x in masked[h]] for h in heads]
        den = [functools.reduce(jnp.add, [jnp.sum(p, axis=0, keepdims=True) for p in probs[h]]) for h in heads]
        outs = [[lax.dot_general(probs[h][n].astype(BF16), vb_ref[n * BLK:(n + 1) * BLK, cols[h]],
                                 (((0,), (0,)), ((), ())), preferred_element_type=F32) for n in blocks]
                for h in heads]
        for h in heads:
            inv = jnp.transpose(jnp.broadcast_to(1.0 / den[h], (HEAD_DIM, BLK)))
            o_ref[0, :, cols[h]] = (functools.reduce(jnp.add, outs[h]) * inv).astype(o_ref.dtype)

    for nb in range(2, NB + 1, 2):
        @pl.when((i >= nb - 2) & (i < nb))
        def _(nb=nb):
            attend(nb)


def moba_mixer(P, cos, sin):
    B, S, _ = P.shape
    BLK = MOBA_BLOCK
    W = MOBA_GROUP * HEAD_DIM
    full = lambda base: pl.BlockSpec((1, S, W), lambda b, h, i: (b, 0, base // W + h))
    tab = pl.BlockSpec((1, S, HEAD_DIM), lambda b, h, i: (b, 0, 0))
    return pl.pallas_call(
        _moba_kernel,
        out_shape=jax.ShapeDtypeStruct((B, S, GROUP_W), BF16),
        grid=(B, MOBA_HEADS // MOBA_GROUP, S // BLK),
        in_specs=[
            pl.BlockSpec((1, BLK, W), lambda b, h, i: (b, i, COL_MQ // W + h)),
            full(COL_MK), full(COL_MV), tab, tab,
        ],
        out_specs=pl.BlockSpec((1, BLK, W), lambda b, h, i: (b, i, h)),
        scratch_shapes=[
            pltpu.VMEM((S, W), BF16),
            pltpu.VMEM((S, W), BF16),
            pltpu.VMEM((MOBA_GROUP, S // BLK, HEAD_DIM), F32),
        ],
        compiler_params=_params(("parallel", "parallel", "arbitrary")),
        name="moba",
    )(P, P, P, cos, sin)


SB_TILE = 256
SB_GROUP = 4


def _sb_kernel(q_ref, k_ref, v_ref, o_ref, qb_ref, run_ref, acc_ref):
    i = pl.program_id(2)
    T = SB_TILE
    qb_ref[...] = (q_ref[0] * (HEAD_DIM ** -0.5)).astype(BF16)
    srow = lax.broadcasted_iota(jnp.int32, (T, T), 0)
    scol = lax.broadcasted_iota(jnp.int32, (T, T), 1)
    upper = jnp.where(srow > scol, 1.0, 0.0).astype(BF16)
    strict = scol < srow

    def tile(j, diagonal):
        rows = pl.ds(pl.multiple_of(j * T, T), T)
        heads = range(SB_GROUP)
        cols = [slice(h * HEAD_DIM, (h + 1) * HEAD_DIM) for h in heads]
        z = [lax.dot_general(qb_ref[:, cols[h]], k_ref[0, rows, cols[h]].astype(BF16), (((1,), (1,)), ((), ())),
                             preferred_element_type=F32) for h in heads]
        soft = [jnp.log(1.0 + jnp.exp(-jnp.abs(z[h]))) for h in heads]
        log_keep = [-jnp.maximum(z[h], 0.0) - soft[h] for h in heads]
        log_beta = [log_keep[h] + z[h] for h in heads]
        if diagonal:
            log_keep = [jnp.where(strict, log_keep[h], 0.0) for h in heads]
        suffix = [jnp.dot(log_keep[h].astype(BF16), upper, preferred_element_type=F32) for h in heads]
        total = [suffix[h][:, 0:1] + log_keep[h][:, 0:1] for h in heads]
        if diagonal:
            w = [jnp.where(strict, jnp.exp(log_beta[h] + suffix[h]), 0.0) for h in heads]
        else:
            w = [jnp.exp(log_beta[h] + (suffix[h] + run_ref[h])) for h in heads]
        pv = [jnp.dot(w[h].astype(BF16), v_ref[0, rows, cols[h]].astype(BF16), preferred_element_type=F32)
              for h in heads]
        for h in heads:
            if diagonal:
                acc_ref[h] = pv[h]
                run_ref[h] = total[h]
            else:
                acc_ref[h] += pv[h]
                run_ref[h] += total[h]

    tile(i, True)

    def body(t, carry):
        tile(i - 1 - t, False)
        return carry

    lax.fori_loop(0, i, body, 0)
    for h in range(SB_GROUP):
        o_ref[0, :, h * HEAD_DIM:(h + 1) * HEAD_DIM] = acc_ref[h].astype(o_ref.dtype)


def sb_mixer(P):
    B, S, _ = P.shape
    T = SB_TILE
    W = SB_GROUP * HEAD_DIM
    full = lambda base: pl.BlockSpec((1, S, W), lambda b, h, i: (b, 0, base // W + h))
    return pl.pallas_call(
        _sb_kernel,
        out_shape=jax.ShapeDtypeStruct((B, S, GROUP_W), BF16),
        grid=(B, SB_HEADS // SB_GROUP, S // T),
        in_specs=[
            pl.BlockSpec((1, T, W), lambda b, h, i: (b, i, COL_BQ // W + h)),
            full(COL_BK), full(COL_BV),
        ],
        out_specs=pl.BlockSpec((1, T, W), lambda b, h, i: (b, i, h)),
        scratch_shapes=[
            pltpu.VMEM((T, W), BF16),
            pltpu.VMEM((SB_GROUP, T, 1), F32),
            pltpu.VMEM((SB_GROUP, T, HEAD_DIM), F32),
        ],
        compiler_params=_params(("parallel", "parallel", "arbitrary")),
        name="stick_breaking",
    )(P, P, P)


CAST_ROWS = 256
CAST_BLOCK_BYTES = 6 * 1024 * 1024


def _cast_kernel(w_ref, o_ref):
    o_ref[...] = w_ref[...].astype(o_ref.dtype)


def cast_bf16(w, layer):
    _, R, C = w.shape
    tr = min(CAST_ROWS, R)
    tc = next(C // k for k in range(1, C // LANES + 1)
              if C % k == 0 and (C // k) % LANES == 0 and tr * (C // k) * 4 <= CAST_BLOCK_BYTES)
    return pl.pallas_call(
        _cast_kernel,
        out_shape=jax.ShapeDtypeStruct((R, C), BF16),
        grid=(R // tr, C // tc),
        in_specs=[pl.BlockSpec((None, tr, tc), lambda i, j: (layer, i, j))],
        out_specs=pl.BlockSpec((tr, tc), lambda i, j: (i, j)),
        compiler_params=_params(("parallel", "parallel")),
        name="cast_bf16",
    )(w)


RELAYOUT_COLS = 256


def _relayout_kernel(a_ref, b_ref, o_ref):
    ob = pl.program_id(0)
    lr_blk = (COL_GG + GROUP_W) // RELAYOUT_COLS
    end_blk = COL_LR // RELAYOUT_COLS
    rank = GLA_GATE_RANK

    @pl.when(ob < lr_blk)
    def _():
        o_ref[...] = jnp.transpose(a_ref[...]).astype(BF16)

    @pl.when((ob >= lr_blk) & (ob < end_blk))
    def _():
        w = jnp.concatenate([a_ref[rank:, :], b_ref[:rank, :]], axis=0)
        o_ref[...] = jnp.transpose(w).astype(BF16)

    @pl.when(ob == end_blk)
    def _():
        w = jnp.concatenate([a_ref[:rank, :], jnp.zeros((RELAYOUT_COLS - rank, a_ref.shape[1]), F32)], axis=0)
        o_ref[...] = jnp.transpose(w).astype(BF16)

    @pl.when(ob > end_blk)
    def _():
        o_ref[...] = jnp.zeros(o_ref.shape, BF16)


def relayout_w_in(w, layer):
    _, R, C = w.shape
    wt = jnp.swapaxes(w, 1, 2)
    lr_blk = (COL_GG + GROUP_W) // RELAYOUT_COLS
    end_blk = COL_LR // RELAYOUT_COLS
    blk = lambda f: pl.BlockSpec((None, RELAYOUT_COLS, R), f)
    return pl.pallas_call(
        _relayout_kernel,
        out_shape=jax.ShapeDtypeStruct((R, NP), BF16),
        grid=(NP // RELAYOUT_COLS,),
        in_specs=[
            blk(lambda ob: (layer, jnp.where(ob == end_blk, lr_blk, jnp.minimum(ob, end_blk)), 0)),
            blk(lambda ob: (layer, jnp.minimum(ob + 1, end_blk), 0)),
        ],
        out_specs=pl.BlockSpec((R, RELAYOUT_COLS), lambda ob: (0, ob)),
        compiler_params=_params(("parallel",)),
        name="relayout_w_in",
    )(wt, wt)


def token_mixing(x2, B, S, cos, sin, w_in_r, wg_pad, bg, ng, sinks, w_out_b, ln_g, ln_b):
    P = in_projection(x2, w_in_r).reshape(B, S, NP)
    ys = (gla_mixer(P, wg_pad, bg, ng), swa_mixer(P, cos, sin, sinks), moba_mixer(P, cos, sin), sb_mixer(P))
    ys = [y.reshape(B * S, GROUP_W) for y in ys]
    return out_projection(x2, ys, w_out_b, ln_g, ln_b)


def kernel(x, positions, w_in, gla_w_gate_up, gla_b_gate_up, gla_norm_g, swa_sinks, w_out,
           ffn1_w_gu, ffn1_w_down, ffn2_w_gu, ffn2_w_down, ln_g, ln_b):
    B, S, D = x.shape
    cos, sin = rope_tables(positions)
    x2 = x.reshape(B * S, D)
    for l in range(DEPTH):
        g = ln_g[l].reshape(3, 1, D)
        b = ln_b[l].reshape(3, 1, D)
        x2 = ffn_sublayer(x2, cast_bf16(ffn1_w_gu, l), cast_bf16(ffn1_w_down, l), g[0], b[0])
        wg_pad = jnp.zeros((LANES, GLA_KEY), F32).at[:GLA_GATE_RANK].set(gla_w_gate_up[l])
        x2 = token_mixing(x2, B, S, cos, sin, relayout_w_in(w_in, l), wg_pad,
                          gla_b_gate_up[l].reshape(1, GLA_KEY), gla_norm_g[l].reshape(1, GLA_DV),
                          swa_sinks[l], cast_bf16(w_out, l), g[1], b[1])
        x2 = ffn_sublayer(x2, cast_bf16(ffn2_w_gu, l), cast_bf16(ffn2_w_down, l), g[2], b[2])
    return x2.reshape(B, S, D)
```

```python
import functools

import jax
import jax.numpy as jnp
from jax import lax
from jax.experimental import pallas as pl
from jax.experimental.pallas import tpu as pltpu

F32 = jnp.float32
BF16 = jnp.bfloat16

D_MODEL = 4096
DEPTH = 2
GROUP_W = 1024
HEAD_DIM = 128
GLA_HEADS = 4
GLA_DV = 256
GLA_DK = 128
GLA_KEY = 512
GLA_GATE_RANK = 16
GLA_GATE_NORMALIZER = 16.0
GLA_CHUNK = 64
SWA_HEADS = 8
SWA_KV_HEADS = 2
SWA_GROUP = SWA_HEADS // SWA_KV_HEADS
SWA_WINDOW = 128
SWA_BLOCK = 128
MOBA_HEADS = 8
MOBA_BLOCK = 256
MOBA_TOPK = 3
MOBA_GROUP = 2
SB_HEADS = 8
ROPE_THETA = 10000.0
D_FF = 11008
FFN_RES = 0.5
LN_EPS = 1e-5
RMS_EPS = 1e-5
DN_ALPHA = (2 * DEPTH) ** 0.25

LANES = 128
VMEM_LIMIT = 56 * 1024 * 1024

COL_GQ, COL_GK, COL_GV, COL_GG = 0, 512, 1024, 2048
COL_SQ, COL_SK, COL_SV = 3072, 4096, 4352
COL_MQ, COL_MK, COL_MV = 4608, 5632, 6656
COL_BQ, COL_BK, COL_BV = 7680, 8704, 9728
COL_LR = 10752
PROJ_TN = 1024
NP = 11264

NEG_INF = float("-inf")


def _params(sem, vmem=VMEM_LIMIT):
    return pltpu.CompilerParams(dimension_semantics=sem, vmem_limit_bytes=vmem)


def _layer_norm_rows(y, g, b):
    mu = jnp.mean(y, axis=-1, keepdims=True)
    yc = y - mu
    var = jnp.mean(yc * yc, axis=-1, keepdims=True)
    return yc * lax.rsqrt(var + LN_EPS) * g + b


LN_ROWS = 32
ACC_COLS = 512


def _residual_layer_norm(o_ref, x_ref, res_scale, g_ref, b_ref):
    def body(r, carry):
        rows = pl.ds(pl.multiple_of(r * LN_ROWS, LN_ROWS), LN_ROWS)
        y = o_ref[rows, :]
        if x_ref is not None:
            y = DN_ALPHA * x_ref[rows, :] + res_scale * y
        o_ref[rows, :] = _layer_norm_rows(y, g_ref[...], b_ref[...])
        return carry

    lax.fori_loop(0, o_ref.shape[0] // LN_ROWS, body, 0)


def _accumulate_dot(o_ref, a, w_ref):
    for c in range(0, o_ref.shape[1], ACC_COLS):
        o_ref[:, c:c + ACC_COLS] += jnp.dot(a, w_ref[:, c:c + ACC_COLS], preferred_element_type=F32)


def _rope_kernel(pos_ref, inv_ref, sign_ref, cos_ref, sin_ref):
    ang = pos_ref[0].astype(F32) * inv_ref[...]
    cos_ref[0] = jnp.cos(ang)
    sin_ref[0] = jnp.sin(ang) * sign_ref[...]


def rope_tables(positions):
    B, S = positions.shape
    ts = min(S, 512)
    inv = 1.0 / (ROPE_THETA ** (jnp.arange(0, HEAD_DIM, 2, dtype=F32) / HEAD_DIM))
    inv_full = jnp.concatenate([inv, inv]).reshape(1, HEAD_DIM)
    sign = jnp.concatenate([-jnp.ones((HEAD_DIM // 2,), F32), jnp.ones((HEAD_DIM // 2,), F32)]).reshape(1, HEAD_DIM)
    pos_b = jnp.broadcast_to(positions[:, :, None], (B, S, HEAD_DIM))
    blk = pl.BlockSpec((1, ts, HEAD_DIM), lambda b, s: (b, s, 0))
    vec = pl.BlockSpec((1, HEAD_DIM), lambda b, s: (0, 0))
    return pl.pallas_call(
        _rope_kernel,
        out_shape=(jax.ShapeDtypeStruct((B, S, HEAD_DIM), F32),) * 2,
        grid=(B, S // ts),
        in_specs=[blk, vec, vec],
        out_specs=(blk, blk),
        compiler_params=_params(("parallel", "parallel")),
        name="rope_tables",
    )(pos_b, inv_full, sign)


def _rope(x, cos, sin):
    return x * cos + pltpu.roll(x, HEAD_DIM // 2, axis=1) * sin


FFN_TM = 512
FFN_TF = 256


def _ffn_kernel(x_ref, wg_ref, wu_ref, wd_ref, g_ref, b_ref, o_ref, xb_ref):
    j = pl.program_id(1)

    @pl.when(j == 0)
    def _():
        xb_ref[...] = x_ref[...].astype(BF16)
        o_ref[...] = jnp.zeros_like(o_ref)

    xb = xb_ref[...]
    gate = jnp.dot(xb, wg_ref[...], preferred_element_type=F32)
    up = jnp.dot(xb, wu_ref[...], preferred_element_type=F32)
    act = (gate / (1.0 + jnp.exp(-gate)) * up).astype(BF16)
    _accumulate_dot(o_ref, act, wd_ref)

    @pl.when(j == pl.num_programs(1) - 1)
    def _():
        _residual_layer_norm(o_ref, x_ref, FFN_RES, g_ref, b_ref)


def ffn_sublayer(x, w_gu, w_down, ln_g, ln_b):
    T, D = x.shape
    tm = min(FFN_TM, T)
    nj = D_FF // FFN_TF
    vec = pl.BlockSpec((1, D), lambda i, j: (0, 0))
    return pl.pallas_call(
        _ffn_kernel,
        out_shape=jax.ShapeDtypeStruct((T, D), F32),
        grid=(T // tm, nj),
        in_specs=[
            pl.BlockSpec((tm, D), lambda i, j: (i, 0)),
            pl.BlockSpec((D, FFN_TF), lambda i, j: (0, j)),
            pl.BlockSpec((D, FFN_TF), lambda i, j: (0, j + nj)),
            pl.BlockSpec((FFN_TF, D), lambda i, j: (j, 0)),
            vec, vec,
        ],
        out_specs=pl.BlockSpec((tm, D), lambda i, j: (i, 0)),
        scratch_shapes=[pltpu.VMEM((tm, D), BF16)],
        compiler_params=_params(("parallel", "arbitrary")),
        name="ffn",
    )(x, w_gu, w_gu, w_down, ln_g, ln_b)


PROJ_TM = 512


def _proj_kernel(x_ref, w_ref, o_ref, xb_ref):
    @pl.when(pl.program_id(1) == 0)
    def _():
        xb_ref[...] = x_ref[...].astype(BF16)

    o_ref[...] = jnp.dot(xb_ref[...], w_ref[...], preferred_element_type=F32)


def in_projection(x, w_in_r):
    T, D = x.shape
    tm = min(PROJ_TM, T)
    return pl.pallas_call(
        _proj_kernel,
        out_shape=jax.ShapeDtypeStruct((T, NP), F32),
        grid=(T // tm, NP // PROJ_TN),
        in_specs=[
            pl.BlockSpec((tm, D), lambda i, j: (i, 0)),
            pl.BlockSpec((D, PROJ_TN), lambda i, j: (0, j)),
        ],
        out_specs=pl.BlockSpec((tm, PROJ_TN), lambda i, j: (i, j)),
        scratch_shapes=[pltpu.VMEM((tm, D), BF16)],
        compiler_params=_params(("parallel", "arbitrary")),
        name="in_proj",
    )(x, w_in_r)


OUT_TM = 256


def _outproj_kernel(x_ref, y0_ref, y1_ref, y2_ref, y3_ref, w_ref, g_ref, b_ref, o_ref):
    for c in range(0, o_ref.shape[1], ACC_COLS):
        cols = slice(c, c + ACC_COLS)
        acc = DN_ALPHA * x_ref[:, cols]
        for idx, y_ref in enumerate((y0_ref, y1_ref, y2_ref, y3_ref)):
            acc = acc + jnp.dot(y_ref[...], w_ref[idx * GROUP_W:(idx + 1) * GROUP_W, cols],
                                preferred_element_type=F32)
        o_ref[:, cols] = acc
    _residual_layer_norm(o_ref, None, 1.0, g_ref, b_ref)


def out_projection(x, ys, w_out, ln_g, ln_b):
    T, D = x.shape
    tm = min(OUT_TM, T)
    vec = pl.BlockSpec((1, D), lambda i: (0, 0))
    yspec = pl.BlockSpec((tm, GROUP_W), lambda i: (i, 0))
    return pl.pallas_call(
        _outproj_kernel,
        out_shape=jax.ShapeDtypeStruct((T, D), F32),
        grid=(T // tm,),
        in_specs=[
            pl.BlockSpec((tm, D), lambda i: (i, 0)),
            yspec, yspec, yspec, yspec,
            pl.BlockSpec(w_out.shape, lambda i: (0, 0), pipeline_mode=pl.Buffered(1)),
            vec, vec,
        ],
        out_specs=pl.BlockSpec((tm, D), lambda i: (i, 0)),
        compiler_params=_params(("parallel",)),
        name="out_proj",
    )(x, *ys, w_out, ln_g, ln_b)


GLA_ROWS = 512
SUBLANES = 8
GLA_SUB = 16
GLA_GROUP = 4


def _cumsum_rows(x):
    n = x.shape[0]
    row = lax.broadcasted_iota(jnp.int32, x.shape, 0)
    sh = 1
    while sh < n:
        x = x + jnp.where(row >= sh, pltpu.roll(x, sh, axis=0), 0.0)
        sh *= 2
    return x


def _gla_kernel(q_ref, k_ref, v_ref, gg_ref, lr_ref, wg_ref, bg_ref, ng_ref, o_ref,
                st_ref, b_scr, k_scr, v_scr):
    C = GLA_CHUNK
    heads = range(GLA_GROUP)
    kcols = [slice(h * GLA_DK, (h + 1) * GLA_DK) for h in heads]
    vcols = [slice(h * GLA_DV, (h + 1) * GLA_DV) for h in heads]
    nt = (((1,), (1,)), ((), ()))

    @pl.when(pl.program_id(2) == 0)
    def _():
        st_ref[...] = jnp.zeros_like(st_ref)

    n_chunks = q_ref.shape[1] // C
    row8 = lax.broadcasted_iota(jnp.int32, (SUBLANES, 1), 0)

    def chunk(c, carry):
        r0 = pl.multiple_of(c * C, C)
        rows = pl.ds(r0, C)
        lr = lr_ref[0, rows, :]
        logits = [jnp.dot(lr, wg_ref[:, kcols[h]], preferred_element_type=F32,
                          precision=lax.Precision.HIGHEST) + bg_ref[:, kcols[h]] for h in heads]
        q = [q_ref[0, rows, kcols[h]] * (GLA_DK ** -0.5) for h in heads]
        k = [k_ref[0, rows, kcols[h]] for h in heads]
        v = [v_ref[0, rows, vcols[h]] for h in heads]
        g = [(jnp.minimum(x, 0.0) - jnp.log(1.0 + jnp.exp(-jnp.abs(x)))) / GLA_GATE_NORMALIZER for x in logits]
        b = [_cumsum_rows(x) for x in g]
        b_last = [x[C - 1:C, :] for x in b]
        for h in heads:
            b_scr[h] = b[h]
            k_scr[h] = k[h]
            v_scr[h] = v[h]
        st = [st_ref[h] for h in heads]
        o_inter = [lax.dot_general((q[h] * jnp.exp(b[h])).astype(BF16), st[h].astype(BF16), nt,
                                   preferred_element_type=F32) for h in heads]
        v_b = [x.astype(BF16) for x in v]
        for h in heads:
            kd = (k[h] * jnp.exp(b_last[h] - b[h])).astype(BF16)
            st_ref[h] = st[h] * jnp.exp(b_last[h]) + lax.dot_general(
                v_b[h], kd, (((0,), (0,)), ((), ())), preferred_element_type=F32)

        for s in range(0, C, GLA_SUB):
            blk_acc = [o_inter[h][s:s + GLA_SUB] for h in heads]
            if s > 0:
                att = []
                for h in heads:
                    b_s = b_scr[h, s:s + 1, :]
                    q_t = (q[h][s:s + GLA_SUB] * jnp.exp(b[h][s:s + GLA_SUB] - b_s)).astype(BF16)
                    k_t = jnp.concatenate([k[h][:s] * jnp.exp(b_s - b[h][:s]), jnp.zeros((C - s, GLA_DK), F32)],
                                          axis=0).astype(BF16)
                    att.append(lax.dot_general(q_t, k_t, nt, preferred_element_type=F32))
                blk_acc = [blk_acc[h] + jnp.dot(att[h].astype(BF16), v_b[h], preferred_element_type=F32)
                           for h in heads]
            for i0 in range(s, s + GLA_SUB, SUBLANES):
                for h in heads:
                    q_i = q[h][i0:i0 + SUBLANES]
                    b_i = b[h][i0:i0 + SUBLANES]
                    acc = blk_acc[h][i0 - s:i0 - s + SUBLANES]
                    for j in range(s, i0 + SUBLANES):
                        b_j = b_scr[h, j:j + 1, :]
                        k_j = k_scr[h, j:j + 1, :]
                        v_j = v_scr[h, j:j + 1, :]
                        e = jnp.exp(jnp.minimum(b_i - b_j, 0.0))
                        a = jnp.sum(q_i * k_j * e, axis=-1, keepdims=True)
                        if j >= i0:
                            a = jnp.where(row8 >= (j - i0), a, 0.0)
                        acc = acc + a * v_j
                    acc = acc * lax.rsqrt(jnp.mean(acc * acc, axis=-1, keepdims=True) + RMS_EPS) * ng_ref[...]
                    gg = gg_ref[0, pl.ds(r0 + i0, SUBLANES), vcols[h]]
                    o_ref[0, pl.ds(r0 + i0, SUBLANES), vcols[h]] = (
                        acc * (gg / (1.0 + jnp.exp(-gg)))).astype(o_ref.dtype)
        return carry

    lax.fori_loop(0, n_chunks, chunk, 0)


def gla_mixer(P, wg_pad, bg, ng):
    B, S, _ = P.shape
    R = min(GLA_ROWS, S)
    kw = GLA_GROUP * GLA_DK
    vw = GLA_GROUP * GLA_DV
    kb = lambda base: (lambda b, h, r: (b, r, base // kw + h))
    vb = lambda base: (lambda b, h, r: (b, r, base // vw + h))
    return pl.pallas_call(
        _gla_kernel,
        out_shape=jax.ShapeDtypeStruct((B, S, GROUP_W), BF16),
        grid=(B, GLA_HEADS // GLA_GROUP, S // R),
        in_specs=[
            pl.BlockSpec((1, R, kw), kb(COL_GQ)),
            pl.BlockSpec((1, R, kw), kb(COL_GK)),
            pl.BlockSpec((1, R, vw), vb(COL_GV)),
            pl.BlockSpec((1, R, vw), vb(COL_GG)),
            pl.BlockSpec((1, R, LANES), lambda b, h, r: (b, r, COL_LR // LANES)),
            pl.BlockSpec((LANES, kw), lambda b, h, r: (0, h)),
            pl.BlockSpec((1, kw), lambda b, h, r: (0, h)),
            pl.BlockSpec((1, GLA_DV), lambda b, h, r: (0, 0)),
        ],
        out_specs=pl.BlockSpec((1, R, vw), lambda b, h, r: (b, r, h)),
        scratch_shapes=[
            pltpu.VMEM((GLA_GROUP, GLA_DV, GLA_DK), F32),
            pltpu.VMEM((GLA_GROUP, GLA_CHUNK, GLA_DK), F32),
            pltpu.VMEM((GLA_GROUP, GLA_CHUNK, GLA_DK), F32),
            pltpu.VMEM((GLA_GROUP, GLA_CHUNK, GLA_DV), F32),
        ],
        compiler_params=_params(("parallel", "parallel", "arbitrary")),
        name="gla",
    )(P, P, P, P, P, wg_pad, bg, ng)


def _swa_kernel(sink_ref, q_ref, kc_ref, kp_ref, vc_ref, vp_ref, cc_ref, sc_ref, cp_ref, sp_ref, o_ref):
    n = pl.program_id(1)
    BLK = SWA_BLOCK
    nt = (((1,), (1,)), ((), ()))
    cos, sin = cc_ref[0], sc_ref[0]
    cos_p, sin_p = cp_ref[0], sp_ref[0]
    qpos = BLK + lax.broadcasted_iota(jnp.int32, (BLK, 2 * BLK), 0)
    kpos = lax.broadcasted_iota(jnp.int32, (BLK, 2 * BLK), 1)
    rel = qpos - kpos
    mask = (rel >= 0) & (rel < SWA_WINDOW) & ((kpos >= BLK) | (n > 0))
    kvs = range(SWA_KV_HEADS)
    hd = lambda h: slice(h * HEAD_DIM, (h + 1) * HEAD_DIM)
    k_cat = [jnp.concatenate([_rope(kp_ref[0, :, hd(kv)], cos_p, sin_p), _rope(kc_ref[0, :, hd(kv)], cos, sin)],
                             axis=0).astype(BF16) for kv in kvs]
    v_cat = [jnp.concatenate([vp_ref[0, :, hd(kv)], vc_ref[0, :, hd(kv)]], axis=0).astype(BF16) for kv in kvs]
    heads = range(SWA_HEADS)
    q = [_rope(q_ref[0, :, hd(h)], cos, sin).astype(BF16) for h in heads]
    s = [lax.dot_general(q[h], k_cat[h // SWA_GROUP], nt, preferred_element_type=F32) * (HEAD_DIM ** -0.5)
         for h in heads]
    s = [jnp.where(mask, x, NEG_INF) for x in s]
    sink = [sink_ref[h] for h in heads]
    m = [jnp.maximum(jnp.max(s[h], axis=-1, keepdims=True), sink[h]) for h in heads]
    p = [jnp.exp(s[h] - m[h]) for h in heads]
    den = [jnp.sum(p[h], axis=-1, keepdims=True) + jnp.exp(sink[h] - m[h]) for h in heads]
    o = [jnp.dot(p[h].astype(BF16), v_cat[h // SWA_GROUP], preferred_element_type=F32) / den[h] for h in heads]
    for h in heads:
        o_ref[0, :, hd(h)] = o[h].astype(o_ref.dtype)


def swa_mixer(P, cos, sin, sinks):
    B, S, _ = P.shape
    BLK = SWA_BLOCK
    kvw = SWA_KV_HEADS * HEAD_DIM
    cur = lambda base, w: (lambda b, n, s: (b, n, base // w))
    prev = lambda base, w: (lambda b, n, s: (b, jnp.maximum(n - 1, 0), base // w))
    kvb = lambda f: pl.BlockSpec((1, BLK, kvw), f)
    tab = lambda f: pl.BlockSpec((1, BLK, HEAD_DIM), f)
    return pl.pallas_call(
        _swa_kernel,
        out_shape=jax.ShapeDtypeStruct((B, S, GROUP_W), BF16),
        grid_spec=pltpu.PrefetchScalarGridSpec(
            num_scalar_prefetch=1,
            grid=(B, S // BLK),
            in_specs=[
                pl.BlockSpec((1, BLK, GROUP_W), cur(COL_SQ, GROUP_W)),
                kvb(cur(COL_SK, kvw)), kvb(prev(COL_SK, kvw)), kvb(cur(COL_SV, kvw)), kvb(prev(COL_SV, kvw)),
                tab(cur(0, HEAD_DIM)), tab(cur(0, HEAD_DIM)), tab(prev(0, HEAD_DIM)), tab(prev(0, HEAD_DIM)),
            ],
            out_specs=pl.BlockSpec((1, BLK, GROUP_W), lambda b, n, s: (b, n, 0)),
        ),
        compiler_params=_params(("parallel", "arbitrary")),
        name="swa",
    )(sinks, P, P, P, P, P, cos, sin, cos, sin)


def _moba_kernel(q_ref, k_ref, v_ref, cos_ref, sin_ref, o_ref, kr_ref, vb_ref, kbar_ref):
    i = pl.program_id(2)
    BLK = MOBA_BLOCK
    S = k_ref.shape[1]
    NB = S // BLK
    heads = range(MOBA_GROUP)
    cols = [slice(h * HEAD_DIM, (h + 1) * HEAD_DIM) for h in heads]
    nt = (((1,), (1,)), ((), ()))

    @pl.when(i == 0)
    def _():
        for h in heads:
            kr = _rope(k_ref[0, :, cols[h]], cos_ref[0], sin_ref[0])
            kr_ref[:, cols[h]] = kr.astype(BF16)
            vb_ref[:, cols[h]] = v_ref[0, :, cols[h]].astype(BF16)
            for n in range(NB):
                kbar_ref[h, n:n + 1, :] = jnp.mean(kr[n * BLK:(n + 1) * BLK], axis=0, keepdims=True)

    rows = pl.ds(pl.multiple_of(i * BLK, BLK), BLK)
    cos_q, sin_q = cos_ref[0, rows, :], sin_ref[0, rows, :]
    q = [_rope(q_ref[0, :, cols[h]], cos_q, sin_q) for h in heads]

    gate = [lax.dot_general(kbar_ref[h], q[h], nt, preferred_element_type=F32,
                            precision=lax.Precision.HIGHEST) for h in heads]
    blk = lax.broadcasted_iota(jnp.int32, (NB, BLK), 0)
    past = blk < i
    sel = []
    for h in heads:
        sel_h = []
        for n in range(NB):
            g_n = gate[h][n:n + 1, :]
            ahead = ((gate[h] > g_n) | ((gate[h] == g_n) & (blk < n))) & past
            rank = jnp.sum(jnp.where(ahead, 1.0, 0.0), axis=0, keepdims=True)
            sel_h.append(rank < float(MOBA_TOPK))
        sel.append(sel_h)

    qb = [x.astype(BF16) for x in q]
    key = lax.broadcasted_iota(jnp.int32, (BLK, BLK), 0)
    qry = lax.broadcasted_iota(jnp.int32, (BLK, BLK), 1)
    causal = key <= qry

    def attend(nb):
        blocks = range(nb)
        scores = [[lax.dot_general(kr_ref[n * BLK:(n + 1) * BLK, cols[h]], qb[h], nt,
                                   preferred_element_type=F32) for n in blocks] for h in heads]
        masked = [[jnp.where((sel[h][n] & (n < i)) | (causal & (n == i)),
                             scores[h][n] * (HEAD_DIM ** -0.5), NEG_INF) for n in blocks] for h in heads]
        m = [functools.reduce(jnp.maximum, [jnp.max(x, axis=0, keepdims=True) for x in masked[h]]) for h in heads]
        probs = [[jnp.exp(x - m[h]) for x in masked[h]] for h in heads]
        den = [functools.reduce(jnp.add, [jnp.sum(p, axis=0, keepdims=True) for p in probs[h]]) for h in heads]
        outs = [[lax.dot_general(probs[h][n].astype(BF16), vb_ref[n * BLK:(n + 1) * BLK, cols[h]],
                                 (((0,), (0,)), ((), ())), preferred_element_type=F32) for n in blocks]
                for h in heads]
        for h in heads:
            inv = jnp.transpose(jnp.broadcast_to(1.0 / den[h], (HEAD_DIM, BLK)))
            o_ref[0, :, cols[h]] = (functools.reduce(jnp.add, outs[h]) * inv).astype(o_ref.dtype)

    for nb in range(2, NB + 1, 2):
        @pl.when((i >= nb - 2) & (i < nb))
        def _(nb=nb):
            attend(nb)


def moba_mixer(P, cos, sin):
    B, S, _ = P.shape
    BLK = MOBA_BLOCK
    W = MOBA_GROUP * HEAD_DIM
    full = lambda base: pl.BlockSpec((1, S, W), lambda b, h, i: (b, 0, base // W + h))
    tab = pl.BlockSpec((1, S, HEAD_DIM), lambda b, h, i: (b, 0, 0))
    return pl.pallas_call(
        _moba_kernel,
        out_shape=jax.ShapeDtypeStruct((B, S, GROUP_W), BF16),
        grid=(B, MOBA_HEADS // MOBA_GROUP, S // BLK),
        in_specs=[
            pl.BlockSpec((1, BLK, W), lambda b, h, i: (b, i, COL_MQ // W + h)),
            full(COL_MK), full(COL_MV), tab, tab,
        ],
        out_specs=pl.BlockSpec((1, BLK, W), lambda b, h, i: (b, i, h)),
        scratch_shapes=[
            pltpu.VMEM((S, W), BF16),
            pltpu.VMEM((S, W), BF16),
            pltpu.VMEM((MOBA_GROUP, S // BLK, HEAD_DIM), F32),
        ],
        compiler_params=_params(("parallel", "parallel", "arbitrary")),
        name="moba",
    )(P, P, P, cos, sin)


SB_TILE = 256
SB_GROUP = 4


def _sb_kernel(q_ref, k_ref, v_ref, o_ref, qb_ref, run_ref, acc_ref):
    i = pl.program_id(2)
    T = SB_TILE
    qb_ref[...] = (q_ref[0] * (HEAD_DIM ** -0.5)).astype(BF16)
    srow = lax.broadcasted_iota(jnp.int32, (T, T), 0)
    scol = lax.broadcasted_iota(jnp.int32, (T, T), 1)
    upper = jnp.where(srow > scol, 1.0, 0.0).astype(BF16)
    strict = scol < srow

    def tile(j, diagonal):
        rows = pl.ds(pl.multiple_of(j * T, T), T)
        heads = range(SB_GROUP)
        cols = [slice(h * HEAD_DIM, (h + 1) * HEAD_DIM) for h in heads]
        z = [lax.dot_general(qb_ref[:, cols[h]], k_ref[0, rows, cols[h]].astype(BF16), (((1,), (1,)), ((), ())),
                             preferred_element_type=F32) for h in heads]
        soft = [jnp.log(1.0 + jnp.exp(-jnp.abs(z[h]))) for h in heads]
        log_keep = [-jnp.maximum(z[h], 0.0) - soft[h] for h in heads]
        log_beta = [log_keep[h] + z[h] for h in heads]
        if diagonal:
            log_keep = [jnp.where(strict, log_keep[h], 0.0) for h in heads]
        suffix = [jnp.dot(log_keep[h].astype(BF16), upper, preferred_element_type=F32) for h in heads]
        total = [suffix[h][:, 0:1] + log_keep[h][:, 0:1] for h in heads]
        if diagonal:
            w = [jnp.where(strict, jnp.exp(log_beta[h] + suffix[h]), 0.0) for h in heads]
        else:
            w = [jnp.exp(log_beta[h] + (suffix[h] + run_ref[h])) for h in heads]
        pv = [jnp.dot(w[h].astype(BF16), v_ref[0, rows, cols[h]].astype(BF16), preferred_element_type=F32)
              for h in heads]
        for h in heads:
            if diagonal:
                acc_ref[h] = pv[h]
                run_ref[h] = total[h]
            else:
                acc_ref[h] += pv[h]
                run_ref[h] += total[h]

    tile(i, True)

    def body(t, carry):
        tile(i - 1 - t, False)
        return carry

    lax.fori_loop(0, i, body, 0)
    for h in range(SB_GROUP):
        o_ref[0, :, h * HEAD_DIM:(h + 1) * HEAD_DIM] = acc_ref[h].astype(o_ref.dtype)


def sb_mixer(P):
    B, S, _ = P.shape
    T = SB_TILE
    W = SB_GROUP * HEAD_DIM
    full = lambda base: pl.BlockSpec((1, S, W), lambda b, h, i: (b, 0, base // W + h))
    return pl.pallas_call(
        _sb_kernel,
        out_shape=jax.ShapeDtypeStruct((B, S, GROUP_W), BF16),
        grid=(B, SB_HEADS // SB_GROUP, S // T),
        in_specs=[
            pl.BlockSpec((1, T, W), lambda b, h, i: (b, i, COL_BQ // W + h)),
            full(COL_BK), full(COL_BV),
        ],
        out_specs=pl.BlockSpec((1, T, W), lambda b, h, i: (b, i, h)),
        scratch_shapes=[
            pltpu.VMEM((T, W), BF16),
            pltpu.VMEM((SB_GROUP, T, 1), F32),
            pltpu.VMEM((SB_GROUP, T, HEAD_DIM), F32),
        ],
        compiler_params=_params(("parallel", "parallel", "arbitrary")),
        name="stick_breaking",
    )(P, P, P)


CAST_ROWS = 256
CAST_BLOCK_BYTES = 6 * 1024 * 1024


def _cast_kernel(w_ref, o_ref):
    o_ref[...] = w_ref[...].astype(o_ref.dtype)


def cast_bf16(w, layer):
    _, R, C = w.shape
    tr = min(CAST_ROWS, R)
    tc = next(C // k for k in range(1, C // LANES + 1)
              if C % k == 0 and (C // k) % LANES == 0 and tr * (C // k) * 4 <= CAST_BLOCK_BYTES)
    return pl.pallas_call(
        _cast_kernel,
        out_shape=jax.ShapeDtypeStruct((R, C), BF16),
        grid=(R // tr, C // tc),
        in_specs=[pl.BlockSpec((None, tr, tc), lambda i, j: (layer, i, j))],
        out_specs=pl.BlockSpec((tr, tc), lambda i, j: (i, j)),
        compiler_params=_params(("parallel", "parallel")),
        name="cast_bf16",
    )(w)


RELAYOUT_COLS = 256


def _relayout_kernel(a_ref, b_ref, o_ref):
    ob = pl.program_id(0)
    lr_blk = (COL_GG + GROUP_W) // RELAYOUT_COLS
    end_blk = COL_LR // RELAYOUT_COLS
    rank = GLA_GATE_RANK

    @pl.when(ob < lr_blk)
    def _():
        o_ref[...] = jnp.transpose(a_ref[...]).astype(BF16)

    @pl.when((ob >= lr_blk) & (ob < end_blk))
    def _():
        w = jnp.concatenate([a_ref[rank:, :], b_ref[:rank, :]], axis=0)
        o_ref[...] = jnp.transpose(w).astype(BF16)

    @pl.when(ob == end_blk)
    def _():
        w = jnp.concatenate([a_ref[:rank, :], jnp.zeros((RELAYOUT_COLS - rank, a_ref.shape[1]), F32)], axis=0)
        o_ref[...] = jnp.transpose(w).astype(BF16)

    @pl.when(ob > end_blk)
    def _():
        o_ref[...] = jnp.zeros(o_ref.shape, BF16)


def relayout_w_in(w, layer):
    _, R, C = w.shape
    wt = jnp.swapaxes(w, 1, 2)
    lr_blk = (COL_GG + GROUP_W) // RELAYOUT_COLS
    end_blk = COL_LR // RELAYOUT_COLS
    blk = lambda f: pl.BlockSpec((None, RELAYOUT_COLS, R), f)
    return pl.pallas_call(
        _relayout_kernel,
        out_shape=jax.ShapeDtypeStruct((R, NP), BF16),
        grid=(NP // RELAYOUT_COLS,),
        in_specs=[
            blk(lambda ob: (layer, jnp.where(ob == end_blk, lr_blk, jnp.minimum(ob, end_blk)), 0)),
            blk(lambda ob: (layer, jnp.minimum(ob + 1, end_blk), 0)),
        ],
        out_specs=pl.BlockSpec((R, RELAYOUT_COLS), lambda ob: (0, ob)),
        compiler_params=_params(("parallel",)),
        name="relayout_w_in",
    )(wt, wt)


def token_mixing(x2, B, S, cos, sin, w_in_r, wg_pad, bg, ng, sinks, w_out_b, ln_g, ln_b):
    P = in_projection(x2, w_in_r).reshape(B, S, NP)
    ys = (gla_mixer(P, wg_pad, bg, ng), swa_mixer(P, cos, sin, sinks), moba_mixer(P, cos, sin), sb_mixer(P))
    ys = [y.reshape(B * S, GROUP_W) for y in ys]
    return out_projection(x2, ys, w_out_b, ln_g, ln_b)


def kernel(x, positions, w_in, gla_w_gate_up, gla_b_gate_up, gla_norm_g, swa_sinks, w_out,
           ffn1_w_gu, ffn1_w_down, ffn2_w_gu, ffn2_w_down, ln_g, ln_b):
    B, S, D = x.shape
    cos, sin = rope_tables(positions)
    x2 = x.reshape(B * S, D)
    for l in range(DEPTH):
        g = ln_g[l].reshape(3, 1, D)
        b = ln_b[l].reshape(3, 1, D)
        x2 = ffn_sublayer(x2, cast_bf16(ffn1_w_gu, l), cast_bf16(ffn1_w_down, l), g[0], b[0])
        wg_pad = jnp.zeros((LANES, GLA_KEY), F32).at[:GLA_GATE_RANK].set(gla_w_gate_up[l])
        x2 = token_mixing(x2, B, S, cos, sin, relayout_w_in(w_in, l), wg_pad,
                          gla_b_gate_up[l].reshape(1, GLA_KEY), gla_norm_g[l].reshape(1, GLA_DV),
                          swa_sinks[l], cast_bf16(w_out, l), g[1], b[1])
        x2 = ffn_sublayer(x2, cast_bf16(ffn2_w_gu, l), cast_bf16(ffn2_w_down, l), g[2], b[2])
    return x2.reshape(B, S, D)
```

```python
import functools

import jax
import jax.numpy as jnp
from jax import lax
from jax.experimental import pallas as pl
from jax.experimental.pallas import tpu as pltpu

F32 = jnp.float32
BF16 = jnp.bfloat16

D_MODEL = 4096
DEPTH = 2
GROUP_W = 1024
HEAD_DIM = 128
GLA_HEADS = 4
GLA_DV = 256
GLA_DK = 128
GLA_KEY = 512
GLA_GATE_RANK = 16
GLA_GATE_NORMALIZER = 16.0
GLA_CHUNK = 64
SWA_HEADS = 8
SWA_KV_HEADS = 2
SWA_GROUP = SWA_HEADS // SWA_KV_HEADS
SWA_WINDOW = 128
SWA_BLOCK = 128
MOBA_HEADS = 8
MOBA_BLOCK = 256
MOBA_TOPK = 3
MOBA_GROUP = 2
SB_HEADS = 8
ROPE_THETA = 10000.0
D_FF = 11008
FFN_RES = 0.5
LN_EPS = 1e-5
RMS_EPS = 1e-5
DN_ALPHA = (2 * DEPTH) ** 0.25

LANES = 128
VMEM_LIMIT = 56 * 1024 * 1024

COL_GQ, COL_GK, COL_GV, COL_GG = 0, 512, 1024, 2048
COL_SQ, COL_SK, COL_SV = 3072, 4096, 4352
COL_MQ, COL_MK, COL_MV = 4608, 5632, 6656
COL_BQ, COL_BK, COL_BV = 7680, 8704, 9728
COL_LR = 10752
PROJ_TN = 1024
NP = 11264

NEG_INF = float("-inf")


def _params(sem, vmem=VMEM_LIMIT):
    return pltpu.CompilerParams(dimension_semantics=sem, vmem_limit_bytes=vmem)


def _layer_norm_rows(y, g, b):
    mu = jnp.mean(y, axis=-1, keepdims=True)
    yc = y - mu
    var = jnp.mean(yc * yc, axis=-1, keepdims=True)
    return yc * lax.rsqrt(var + LN_EPS) * g + b


LN_ROWS = 32
ACC_COLS = 512


def _residual_layer_norm(o_ref, x_ref, res_scale, g_ref, b_ref):
    def body(r, carry):
        rows = pl.ds(pl.multiple_of(r * LN_ROWS, LN_ROWS), LN_ROWS)
        y = o_ref[rows, :]
        if x_ref is not None:
            y = DN_ALPHA * x_ref[rows, :] + res_scale * y
        o_ref[rows, :] = _layer_norm_rows(y, g_ref[...], b_ref[...])
        return carry

    lax.fori_loop(0, o_ref.shape[0] // LN_ROWS, body, 0)


def _accumulate_dot(o_ref, a, w_ref):
    for c in range(0, o_ref.shape[1], ACC_COLS):
        o_ref[:, c:c + ACC_COLS] += jnp.dot(a, w_ref[:, c:c + ACC_COLS], preferred_element_type=F32)


def _rope_kernel(pos_ref, inv_ref, sign_ref, cos_ref, sin_ref):
    ang = pos_ref[0].astype(F32) * inv_ref[...]
    cos_ref[0] = jnp.cos(ang)
    sin_ref[0] = jnp.sin(ang) * sign_ref[...]


def rope_tables(positions):
    B, S = positions.shape
    ts = min(S, 512)
    inv = 1.0 / (ROPE_THETA ** (jnp.arange(0, HEAD_DIM, 2, dtype=F32) / HEAD_DIM))
    inv_full = jnp.concatenate([inv, inv]).reshape(1, HEAD_DIM)
    sign = jnp.concatenate([-jnp.ones((HEAD_DIM // 2,), F32), jnp.ones((HEAD_DIM // 2,), F32)]).reshape(1, HEAD_DIM)
    pos_b = jnp.broadcast_to(positions[:, :, None], (B, S, HEAD_DIM))
    blk = pl.BlockSpec((1, ts, HEAD_DIM), lambda b, s: (b, s, 0))
    vec = pl.BlockSpec((1, HEAD_DIM), lambda b, s: (0, 0))
    return pl.pallas_call(
        _rope_kernel,
        out_shape=(jax.ShapeDtypeStruct((B, S, HEAD_DIM), F32),) * 2,
        grid=(B, S // ts),
        in_specs=[blk, vec, vec],
        out_specs=(blk, blk),
        compiler_params=_params(("parallel", "parallel")),
        name="rope_tables",
    )(pos_b, inv_full, sign)


def _rope(x, cos, sin):
    return x * cos + pltpu.roll(x, HEAD_DIM // 2, axis=1) * sin


FFN_TM = 512
FFN_TF = 256


def _ffn_kernel(x_ref, wgu_ref, wd_ref, g_ref, b_ref, o_ref, xb_ref):
    j = pl.program_id(1)

    @pl.when(j == 0)
    def _():
        xb_ref[...] = x_ref[...].astype(BF16)
        o_ref[...] = jnp.zeros_like(o_ref)

    h = jnp.dot(xb_ref[...], wgu_ref[...], preferred_element_type=F32)
    gate = h[:, :FFN_TF]
    up = h[:, FFN_TF:]
    act = (gate / (1.0 + jnp.exp(-gate)) * up).astype(BF16)
    _accumulate_dot(o_ref, act, wd_ref)

    @pl.when(j == pl.num_programs(1) - 1)
    def _():
        _residual_layer_norm(o_ref, x_ref, FFN_RES, g_ref, b_ref)


def ffn_sublayer(x, w_gu, w_down, ln_g, ln_b):
    T, D = x.shape
    tm = min(FFN_TM, T)
    nj = w_gu.shape[0]
    vec = pl.BlockSpec((1, D), lambda i, j: (0, 0))
    return pl.pallas_call(
        _ffn_kernel,
        out_shape=jax.ShapeDtypeStruct((T, D), F32),
        grid=(T // tm, nj),
        in_specs=[
            pl.BlockSpec((tm, D), lambda i, j: (i, 0)),
            pl.BlockSpec((None, D, 2 * FFN_TF), lambda i, j: (j, 0, 0)),
            pl.BlockSpec((FFN_TF, D), lambda i, j: (j, 0)),
            vec, vec,
        ],
        out_specs=pl.BlockSpec((tm, D), lambda i, j: (i, 0)),
        scratch_shapes=[pltpu.VMEM((tm, D), BF16)],
        compiler_params=_params(("parallel", "arbitrary")),
        name="ffn",
    )(x, w_gu, w_down, ln_g, ln_b)


GU_CAST_ROWS = 1024


def _cast_gate_up_kernel(g_ref, u_ref, o_ref):
    o_ref[:, :FFN_TF] = g_ref[...].astype(BF16)
    o_ref[:, FFN_TF:] = u_ref[...].astype(BF16)


def cast_gate_up(w, layer):
    _, R, C = w.shape
    nj = C // 2 // FFN_TF
    tr = min(GU_CAST_ROWS, R)
    return pl.pallas_call(
        _cast_gate_up_kernel,
        out_shape=jax.ShapeDtypeStruct((nj, R, 2 * FFN_TF), BF16),
        grid=(R // tr, nj),
        in_specs=[
            pl.BlockSpec((None, tr, FFN_TF), lambda i, j: (layer, i, j)),
            pl.BlockSpec((None, tr, FFN_TF), lambda i, j: (layer, i, j + nj)),
        ],
        out_specs=pl.BlockSpec((None, tr, 2 * FFN_TF), lambda i, j: (j, i, 0)),
        compiler_params=_params(("parallel", "parallel")),
        name="cast_gate_up",
    )(w, w)


PROJ_TM = 512


def _proj_kernel(x_ref, w_ref, o_ref, xb_ref):
    @pl.when(pl.program_id(1) == 0)
    def _():
        xb_ref[...] = x_ref[...].astype(BF16)

    o_ref[...] = jnp.dot(xb_ref[...], w_ref[...], preferred_element_type=F32)


def in_projection(x, w_in_r):
    T, D = x.shape
    tm = min(PROJ_TM, T)
    return pl.pallas_call(
        _proj_kernel,
        out_shape=jax.ShapeDtypeStruct((T, NP), F32),
        grid=(T // tm, NP // PROJ_TN),
        in_specs=[
            pl.BlockSpec((tm, D), lambda i, j: (i, 0)),
            pl.BlockSpec((D, PROJ_TN), lambda i, j: (0, j)),
        ],
        out_specs=pl.BlockSpec((tm, PROJ_TN), lambda i, j: (i, j)),
        scratch_shapes=[pltpu.VMEM((tm, D), BF16)],
        compiler_params=_params(("parallel", "arbitrary")),
        name="in_proj",
    )(x, w_in_r)


OUT_TM = 256


def _outproj_kernel(x_ref, y0_ref, y1_ref, y2_ref, y3_ref, w_ref, g_ref, b_ref, o_ref):
    for c in range(0, o_ref.shape[1], ACC_COLS):
        cols = slice(c, c + ACC_COLS)
        acc = DN_ALPHA * x_ref[:, cols]
        for idx, y_ref in enumerate((y0_ref, y1_ref, y2_ref, y3_ref)):
            acc = acc + jnp.dot(y_ref[...], w_ref[idx * GROUP_W:(idx + 1) * GROUP_W, cols],
                                preferred_element_type=F32)
        o_ref[:, cols] = acc
    _residual_layer_norm(o_ref, None, 1.0, g_ref, b_ref)


def out_projection(x, ys, w_out, ln_g, ln_b):
    T, D = x.shape
    tm = min(OUT_TM, T)
    vec = pl.BlockSpec((1, D), lambda i: (0, 0))
    yspec = pl.BlockSpec((tm, GROUP_W), lambda i: (i, 0))
    return pl.pallas_call(
        _outproj_kernel,
        out_shape=jax.ShapeDtypeStruct((T, D), F32),
        grid=(T // tm,),
        in_specs=[
            pl.BlockSpec((tm, D), lambda i: (i, 0)),
            yspec, yspec, yspec, yspec,
            pl.BlockSpec(w_out.shape, lambda i: (0, 0), pipeline_mode=pl.Buffered(1)),
            vec, vec,
        ],
        out_specs=pl.BlockSpec((tm, D), lambda i: (i, 0)),
        compiler_params=_params(("parallel",)),
        name="out_proj",
    )(x, *ys, w_out, ln_g, ln_b)


GLA_ROWS = 512
SUBLANES = 8
GLA_SUB = 16
GLA_GROUP = 4


def _cumsum_rows(x):
    n = x.shape[0]
    row = lax.broadcasted_iota(jnp.int32, x.shape, 0)
    sh = 1
    while sh < n:
        x = x + jnp.where(row >= sh, pltpu.roll(x, sh, axis=0), 0.0)
        sh *= 2
    return x


def _gla_kernel(q_ref, k_ref, v_ref, gg_ref, lr_ref, wg_ref, bg_ref, ng_ref, o_ref,
                st_ref, b_scr, k_scr, v_scr):
    C = GLA_CHUNK
    heads = range(GLA_GROUP)
    kcols = [slice(h * GLA_DK, (h + 1) * GLA_DK) for h in heads]
    vcols = [slice(h * GLA_DV, (h + 1) * GLA_DV) for h in heads]
    nt = (((1,), (1,)), ((), ()))

    @pl.when(pl.program_id(2) == 0)
    def _():
        st_ref[...] = jnp.zeros_like(st_ref)

    n_chunks = q_ref.shape[1] // C
    row8 = lax.broadcasted_iota(jnp.int32, (SUBLANES, 1), 0)

    def chunk(c, carry):
        r0 = pl.multiple_of(c * C, C)
        rows = pl.ds(r0, C)
        lr = lr_ref[0, rows, :]
        logits = [jnp.dot(lr, wg_ref[:, kcols[h]], preferred_element_type=F32,
                          precision=lax.Precision.HIGHEST) + bg_ref[:, kcols[h]] for h in heads]
        q = [q_ref[0, rows, kcols[h]] * (GLA_DK ** -0.5) for h in heads]
        k = [k_ref[0, rows, kcols[h]] for h in heads]
        v = [v_ref[0, rows, vcols[h]] for h in heads]
        g = [(jnp.minimum(x, 0.0) - jnp.log(1.0 + jnp.exp(-jnp.abs(x)))) / GLA_GATE_NORMALIZER for x in logits]
        b = [_cumsum_rows(x) for x in g]
        b_last = [x[C - 1:C, :] for x in b]
        for h in heads:
            b_scr[h] = b[h]
            k_scr[h] = k[h]
            v_scr[h] = v[h]
        st = [st_ref[h] for h in heads]
        o_inter = [lax.dot_general((q[h] * jnp.exp(b[h])).astype(BF16), st[h].astype(BF16), nt,
                                   preferred_element_type=F32) for h in heads]
        v_b = [x.astype(BF16) for x in v]
        for h in heads:
            kd = (k[h] * jnp.exp(b_last[h] - b[h])).astype(BF16)
            st_ref[h] = st[h] * jnp.exp(b_last[h]) + lax.dot_general(
                v_b[h], kd, (((0,), (0,)), ((), ())), preferred_element_type=F32)

        for s in range(0, C, GLA_SUB):
            blk_acc = [o_inter[h][s:s + GLA_SUB] for h in heads]
            if s > 0:
                att = []
                for h in heads:
                    b_s = b_scr[h, s:s + 1, :]
                    q_t = (q[h][s:s + GLA_SUB] * jnp.exp(b[h][s:s + GLA_SUB] - b_s)).astype(BF16)
                    k_t = jnp.concatenate([k[h][:s] * jnp.exp(b_s - b[h][:s]), jnp.zeros((C - s, GLA_DK), F32)],
                                          axis=0).astype(BF16)
                    att.append(lax.dot_general(q_t, k_t, nt, preferred_element_type=F32))
                blk_acc = [blk_acc[h] + jnp.dot(att[h].astype(BF16), v_b[h], preferred_element_type=F32)
                           for h in heads]
            for i0 in range(s, s + GLA_SUB, SUBLANES):
                for h in heads:
                    q_i = q[h][i0:i0 + SUBLANES]
                    b_i = b[h][i0:i0 + SUBLANES]
                    acc = blk_acc[h][i0 - s:i0 - s + SUBLANES]
                    for j in range(s, i0 + SUBLANES):
                        b_j = b_scr[h, j:j + 1, :]
                        k_j = k_scr[h, j:j + 1, :]
                        v_j = v_scr[h, j:j + 1, :]
                        e = jnp.exp(jnp.minimum(b_i - b_j, 0.0))
                        a = jnp.sum(q_i * k_j * e, axis=-1, keepdims=True)
                        if j >= i0:
                            a = jnp.where(row8 >= (j - i0), a, 0.0)
                        acc = acc + a * v_j
                    acc = acc * lax.rsqrt(jnp.mean(acc * acc, axis=-1, keepdims=True) + RMS_EPS) * ng_ref[...]
                    gg = gg_ref[0, pl.ds(r0 + i0, SUBLANES), vcols[h]]
                    o_ref[0, pl.ds(r0 + i0, SUBLANES), vcols[h]] = (
                        acc * (gg / (1.0 + jnp.exp(-gg)))).astype(o_ref.dtype)
        return carry

    lax.fori_loop(0, n_chunks, chunk, 0)


def gla_mixer(P, wg_pad, bg, ng):
    B, S, _ = P.shape
    R = min(GLA_ROWS, S)
    kw = GLA_GROUP * GLA_DK
    vw = GLA_GROUP * GLA_DV
    kb = lambda base: (lambda b, h, r: (b, r, base // kw + h))
    vb = lambda base: (lambda b, h, r: (b, r, base // vw + h))
    return pl.pallas_call(
        _gla_kernel,
        out_shape=jax.ShapeDtypeStruct((B, S, GROUP_W), BF16),
        grid=(B, GLA_HEADS // GLA_GROUP, S // R),
        in_specs=[
            pl.BlockSpec((1, R, kw), kb(COL_GQ)),
            pl.BlockSpec((1, R, kw), kb(COL_GK)),
            pl.BlockSpec((1, R, vw), vb(COL_GV)),
            pl.BlockSpec((1, R, vw), vb(COL_GG)),
            pl.BlockSpec((1, R, LANES), lambda b, h, r: (b, r, COL_LR // LANES)),
            pl.BlockSpec((LANES, kw), lambda b, h, r: (0, h)),
            pl.BlockSpec((1, kw), lambda b, h, r: (0, h)),
            pl.BlockSpec((1, GLA_DV), lambda b, h, r: (0, 0)),
        ],
        out_specs=pl.BlockSpec((1, R, vw), lambda b, h, r: (b, r, h)),
        scratch_shapes=[
            pltpu.VMEM((GLA_GROUP, GLA_DV, GLA_DK), F32),
            pltpu.VMEM((GLA_GROUP, GLA_CHUNK, GLA_DK), F32),
            pltpu.VMEM((GLA_GROUP, GLA_CHUNK, GLA_DK), F32),
            pltpu.VMEM((GLA_GROUP, GLA_CHUNK, GLA_DV), F32),
        ],
        compiler_params=_params(("parallel", "parallel", "arbitrary")),
        name="gla",
    )(P, P, P, P, P, wg_pad, bg, ng)


def _swa_kernel(sink_ref, q_ref, kc_ref, kp_ref, vc_ref, vp_ref, cc_ref, sc_ref, cp_ref, sp_ref, o_ref):
    n = pl.program_id(1)
    BLK = SWA_BLOCK
    nt = (((1,), (1,)), ((), ()))
    cos, sin = cc_ref[0], sc_ref[0]
    cos_p, sin_p = cp_ref[0], sp_ref[0]
    qpos = BLK + lax.broadcasted_iota(jnp.int32, (BLK, 2 * BLK), 0)
    kpos = lax.broadcasted_iota(jnp.int32, (BLK, 2 * BLK), 1)
    rel = qpos - kpos
    mask = (rel >= 0) & (rel < SWA_WINDOW) & ((kpos >= BLK) | (n > 0))
    kvs = range(SWA_KV_HEADS)
    hd = lambda h: slice(h * HEAD_DIM, (h + 1) * HEAD_DIM)
    k_cat = [jnp.concatenate([_rope(kp_ref[0, :, hd(kv)], cos_p, sin_p), _rope(kc_ref[0, :, hd(kv)], cos, sin)],
                             axis=0).astype(BF16) for kv in kvs]
    v_cat = [jnp.concatenate([vp_ref[0, :, hd(kv)], vc_ref[0, :, hd(kv)]], axis=0).astype(BF16) for kv in kvs]
    heads = range(SWA_HEADS)
    q = [_rope(q_ref[0, :, hd(h)], cos, sin).astype(BF16) for h in heads]
    s = [lax.dot_general(q[h], k_cat[h // SWA_GROUP], nt, preferred_element_type=F32) * (HEAD_DIM ** -0.5)
         for h in heads]
    s = [jnp.where(mask, x, NEG_INF) for x in s]
    sink = [sink_ref[h] for h in heads]
    m = [jnp.maximum(jnp.max(s[h], axis=-1, keepdims=True), sink[h]) for h in heads]
    p = [jnp.exp(s[h] - m[h]) for h in heads]
    den = [jnp.sum(p[h], axis=-1, keepdims=True) + jnp.exp(sink[h] - m[h]) for h in heads]
    o = [jnp.dot(p[h].astype(BF16), v_cat[h // SWA_GROUP], preferred_element_type=F32) / den[h] for h in heads]
    for h in heads:
        o_ref[0, :, hd(h)] = o[h].astype(o_ref.dtype)


def swa_mixer(P, cos, sin, sinks):
    B, S, _ = P.shape
    BLK = SWA_BLOCK
    kvw = SWA_KV_HEADS * HEAD_DIM
    cur = lambda base, w: (lambda b, n, s: (b, n, base // w))
    prev = lambda base, w: (lambda b, n, s: (b, jnp.maximum(n - 1, 0), base // w))
    kvb = lambda f: pl.BlockSpec((1, BLK, kvw), f)
    tab = lambda f: pl.BlockSpec((1, BLK, HEAD_DIM), f)
    return pl.pallas_call(
        _swa_kernel,
        out_shape=jax.ShapeDtypeStruct((B, S, GROUP_W), BF16),
        grid_spec=pltpu.PrefetchScalarGridSpec(
            num_scalar_prefetch=1,
            grid=(B, S // BLK),
            in_specs=[
                pl.BlockSpec((1, BLK, GROUP_W), cur(COL_SQ, GROUP_W)),
                kvb(cur(COL_SK, kvw)), kvb(prev(COL_SK, kvw)), kvb(cur(COL_SV, kvw)), kvb(prev(COL_SV, kvw)),
                tab(cur(0, HEAD_DIM)), tab(cur(0, HEAD_DIM)), tab(prev(0, HEAD_DIM)), tab(prev(0, HEAD_DIM)),
            ],
            out_specs=pl.BlockSpec((1, BLK, GROUP_W), lambda b, n, s: (b, n, 0)),
        ),
        compiler_params=_params(("parallel", "arbitrary")),
        name="swa",
    )(sinks, P, P, P, P, P, cos, sin, cos, sin)


def _moba_kernel(q_ref, k_ref, v_ref, cos_ref, sin_ref, o_ref, kr_ref, vb_ref, kbar_ref):
    i = pl.program_id(2)
    BLK = MOBA_BLOCK
    S = k_ref.shape[1]
    NB = S // BLK
    heads = range(MOBA_GROUP)
    cols = [slice(h * HEAD_DIM, (h + 1) * HEAD_DIM) for h in heads]
    nt = (((1,), (1,)), ((), ()))

    @pl.when(i == 0)
    def _():
        for h in heads:
            kr = _rope(k_ref[0, :, cols[h]], cos_ref[0], sin_ref[0])
            kr_ref[:, cols[h]] = kr.astype(BF16)
            vb_ref[:, cols[h]] = v_ref[0, :, cols[h]].astype(BF16)
            for n in range(NB):
                kbar_ref[h, n:n + 1, :] = jnp.mean(kr[n * BLK:(n + 1) * BLK], axis=0, keepdims=True)

    rows = pl.ds(pl.multiple_of(i * BLK, BLK), BLK)
    cos_q, sin_q = cos_ref[0, rows, :], sin_ref[0, rows, :]
    q = [_rope(q_ref[0, :, cols[h]], cos_q, sin_q) for h in heads]

    gate = [lax.dot_general(kbar_ref[h], q[h], nt, preferred_element_type=F32,
                            precision=lax.Precision.HIGHEST) for h in heads]
    blk = lax.broadcasted_iota(jnp.int32, (NB, BLK), 0)
    past = blk < i
    sel = []
    for h in heads:
        sel_h = []
        for n in range(NB):
            g_n = gate[h][n:n + 1, :]
            ahead = ((gate[h] > g_n) | ((gate[h] == g_n) & (blk < n))) & past
            rank = jnp.sum(jnp.where(ahead, 1.0, 0.0), axis=0, keepdims=True)
            sel_h.append(rank < float(MOBA_TOPK))
        sel.append(sel_h)

    qb = [x.astype(BF16) for x in q]
    key = lax.broadcasted_iota(jnp.int32, (BLK, BLK), 0)
    qry = lax.broadcasted_iota(jnp.int32, (BLK, BLK), 1)
    causal = key <= qry

    def attend(nb):
        blocks = range(nb)
        scores = [[lax.dot_general(kr_ref[n * BLK:(n + 1) * BLK, cols[h]], qb[h], nt,
                                   preferred_element_type=F32) for n in blocks] for h in heads]
        masked = [[jnp.where((sel[h][n] & (n < i)) | (causal & (n == i)),
                             scores[h][n] * (HEAD_DIM ** -0.5), NEG_INF) for n in blocks] for h in heads]
        m = [functools.reduce(jnp.maximum, [jnp.max(x, axis=0, keepdims=True) for x in masked[h]]) for h in heads]
        probs = [[jnp.exp(x - m[h]) for x in masked[h]] for h in heads]
        den = [functools.reduce(jnp.add, [jnp.sum(p, axis=0, keepdims=True) for p in probs[h]]) for h in heads]
        outs = [[lax.dot_general(probs[h][n].astype(BF16), vb_ref[n * BLK:(n + 1) * BLK, cols[h]],
                                 (((0,), (0,)), ((), ())), preferred_element_type=F32) for n in blocks]
                for h in heads]
        for h in heads:
            inv = jnp.transpose(jnp.broadcast_to(1.0 / den[h], (HEAD_DIM, BLK)))
            o_ref[0, :, cols[h]] = (functools.reduce(jnp.add, outs[h]) * inv).astype(o_ref.dtype)

    for nb in range(2, NB + 1, 2):
        @pl.when((i >= nb - 2) & (i < nb))
        def _(nb=nb):
            attend(nb)


def moba_mixer(P, cos, sin):
    B, S, _ = P.shape
    BLK = MOBA_BLOCK
    W = MOBA_GROUP * HEAD_DIM
    full = lambda base: pl.BlockSpec((1, S, W), lambda b, h, i: (b, 0, base // W + h))
    tab = pl.BlockSpec((1, S, HEAD_DIM), lambda b, h, i: (b, 0, 0))
    return pl.pallas_call(
        _moba_kernel,
        out_shape=jax.ShapeDtypeStruct((B, S, GROUP_W), BF16),
        grid=(B, MOBA_HEADS // MOBA_GROUP, S // BLK),
        in_specs=[
            pl.BlockSpec((1, BLK, W), lambda b, h, i: (b, i, COL_MQ // W + h)),
            full(COL_MK), full(COL_MV), tab, tab,
        ],
        out_specs=pl.BlockSpec((1, BLK, W), lambda b, h, i: (b, i, h)),
        scratch_shapes=[
            pltpu.VMEM((S, W), BF16),
            pltpu.VMEM((S, W), BF16),
            pltpu.VMEM((MOBA_GROUP, S // BLK, HEAD_DIM), F32),
        ],
        compiler_params=_params(("parallel", "parallel", "arbitrary")),
        name="moba",
    )(P, P, P, cos, sin)


SB_TILE = 256
SB_GROUP = 4


def _sb_kernel(q_ref, k_ref, v_ref, o_ref, qb_ref, run_ref, acc_ref):
    i = pl.program_id(2)
    T = SB_TILE
    qb_ref[...] = (q_ref[0] * (HEAD_DIM ** -0.5)).astype(BF16)
    srow = lax.broadcasted_iota(jnp.int32, (T, T), 0)
    scol = lax.broadcasted_iota(jnp.int32, (T, T), 1)
    upper = jnp.where(srow > scol, 1.0, 0.0).astype(BF16)
    strict = scol < srow

    def tile(j, diagonal):
        rows = pl.ds(pl.multiple_of(j * T, T), T)
        heads = range(SB_GROUP)
        cols = [slice(h * HEAD_DIM, (h + 1) * HEAD_DIM) for h in heads]
        z = [lax.dot_general(qb_ref[:, cols[h]], k_ref[0, rows, cols[h]].astype(BF16), (((1,), (1,)), ((), ())),
                             preferred_element_type=F32) for h in heads]
        soft = [jnp.log(1.0 + jnp.exp(-jnp.abs(z[h]))) for h in heads]
        log_keep = [-jnp.maximum(z[h], 0.0) - soft[h] for h in heads]
        log_beta = [log_keep[h] + z[h] for h in heads]
        if diagonal:
            log_keep = [jnp.where(strict, log_keep[h], 0.0) for h in heads]
        hi = [log_keep[h].astype(BF16) for h in heads]
        lo = [(log_keep[h] - hi[h].astype(F32)).astype(BF16) for h in heads]
        suffix = [jnp.dot(hi[h], upper, preferred_element_type=F32)
                  + jnp.dot(lo[h], upper, preferred_element_type=F32) for h in heads]
        total = [suffix[h][:, 0:1] + log_keep[h][:, 0:1] for h in heads]
        if diagonal:
            w = [jnp.where(strict, jnp.exp(log_beta[h] + suffix[h]), 0.0) for h in heads]
        else:
            w = [jnp.exp(log_beta[h] + (suffix[h] + run_ref[h])) for h in heads]
        pv = [jnp.dot(w[h].astype(BF16), v_ref[0, rows, cols[h]].astype(BF16), preferred_element_type=F32)
              for h in heads]
        for h in heads:
            if diagonal:
                acc_ref[h] = pv[h]
                run_ref[h] = total[h]
            else:
                acc_ref[h] += pv[h]
                run_ref[h] += total[h]

    tile(i, True)

    def body(t, carry):
        tile(i - 1 - t, False)
        return carry

    lax.fori_loop(0, i, body, 0)
    for h in range(SB_GROUP):
        o_ref[0, :, h * HEAD_DIM:(h + 1) * HEAD_DIM] = acc_ref[h].astype(o_ref.dtype)


def sb_mixer(P):
    B, S, _ = P.shape
    T = SB_TILE
    W = SB_GROUP * HEAD_DIM
    full = lambda base: pl.BlockSpec((1, S, W), lambda b, h, i: (b, 0, base // W + h))
    return pl.pallas_call(
        _sb_kernel,
        out_shape=jax.ShapeDtypeStruct((B, S, GROUP_W), BF16),
        grid=(B, SB_HEADS // SB_GROUP, S // T),
        in_specs=[
            pl.BlockSpec((1, T, W), lambda b, h, i: (b, i, COL_BQ // W + h)),
            full(COL_BK), full(COL_BV),
        ],
        out_specs=pl.BlockSpec((1, T, W), lambda b, h, i: (b, i, h)),
        scratch_shapes=[
            pltpu.VMEM((T, W), BF16),
            pltpu.VMEM((SB_GROUP, T, 1), F32),
            pltpu.VMEM((SB_GROUP, T, HEAD_DIM), F32),
        ],
        compiler_params=_params(("parallel", "parallel", "arbitrary")),
        name="stick_breaking",
    )(P, P, P)


CAST_ROWS = 256
CAST_BLOCK_BYTES = 6 * 1024 * 1024


def _cast_kernel(w_ref, o_ref):
    o_ref[...] = w_ref[...].astype(o_ref.dtype)


def cast_bf16(w, layer):
    _, R, C = w.shape
    tr = min(CAST_ROWS, R)
    tc = next(C // k for k in range(1, C // LANES + 1)
              if C % k == 0 and (C // k) % LANES == 0 and tr * (C // k) * 4 <= CAST_BLOCK_BYTES)
    return pl.pallas_call(
        _cast_kernel,
        out_shape=jax.ShapeDtypeStruct((R, C), BF16),
        grid=(R // tr, C // tc),
        in_specs=[pl.BlockSpec((None, tr, tc), lambda i, j: (layer, i, j))],
        out_specs=pl.BlockSpec((tr, tc), lambda i, j: (i, j)),
        compiler_params=_params(("parallel", "parallel")),
        name="cast_bf16",
    )(w)


RELAYOUT_COLS = 256


def _relayout_kernel(a_ref, b_ref, o_ref):
    ob = pl.program_id(0)
    lr_blk = (COL_GG + GROUP_W) // RELAYOUT_COLS
    end_blk = COL_LR // RELAYOUT_COLS
    rank = GLA_GATE_RANK

    @pl.when(ob < lr_blk)
    def _():
        o_ref[...] = jnp.transpose(a_ref[...]).astype(BF16)

    @pl.when((ob >= lr_blk) & (ob < end_blk))
    def _():
        w = jnp.concatenate([a_ref[rank:, :], b_ref[:rank, :]], axis=0)
        o_ref[...] = jnp.transpose(w).astype(BF16)

    @pl.when(ob == end_blk)
    def _():
        w = jnp.concatenate([a_ref[:rank, :], jnp.zeros((RELAYOUT_COLS - rank, a_ref.shape[1]), F32)], axis=0)
        o_ref[...] = jnp.transpose(w).astype(BF16)

    @pl.when(ob > end_blk)
    def _():
        o_ref[...] = jnp.zeros(o_ref.shape, BF16)


def relayout_w_in(w, layer):
    _, R, C = w.shape
    wt = jnp.swapaxes(w, 1, 2)
    lr_blk = (COL_GG + GROUP_W) // RELAYOUT_COLS
    end_blk = COL_LR // RELAYOUT_COLS
    blk = lambda f: pl.BlockSpec((None, RELAYOUT_COLS, R), f)
    return pl.pallas_call(
        _relayout_kernel,
        out_shape=jax.ShapeDtypeStruct((R, NP), BF16),
        grid=(NP // RELAYOUT_COLS,),
        in_specs=[
            blk(lambda ob: (layer, jnp.where(ob == end_blk, lr_blk, jnp.minimum(ob, end_blk)), 0)),
            blk(lambda ob: (layer, jnp.minimum(ob + 1, end_blk), 0)),
        ],
        out_specs=pl.BlockSpec((R, RELAYOUT_COLS), lambda ob: (0, ob)),
        compiler_params=_params(("parallel",)),
        name="relayout_w_in",
    )(wt, wt)


def token_mixing(x2, B, S, cos, sin, w_in_r, wg_pad, bg, ng, sinks, w_out_b, ln_g, ln_b):
    P = in_projection(x2, w_in_r).reshape(B, S, NP)
    ys = (gla_mixer(P, wg_pad, bg, ng), swa_mixer(P, cos, sin, sinks), moba_mixer(P, cos, sin), sb_mixer(P))
    ys = [y.reshape(B * S, GROUP_W) for y in ys]
    return out_projection(x2, ys, w_out_b, ln_g, ln_b)


def kernel(x, positions, w_in, gla_w_gate_up, gla_b_gate_up, gla_norm_g, swa_sinks, w_out,
           ffn1_w_gu, ffn1_w_down, ffn2_w_gu, ffn2_w_down, ln_g, ln_b):
    B, S, D = x.shape
    cos, sin = rope_tables(positions)
    x2 = x.reshape(B * S, D)
    for l in range(DEPTH):
        g = ln_g[l].reshape(3, 1, D)
        b = ln_b[l].reshape(3, 1, D)
        x2 = ffn_sublayer(x2, cast_gate_up(ffn1_w_gu, l), cast_bf16(ffn1_w_down, l), g[0], b[0])
        wg_pad = jnp.zeros((LANES, GLA_KEY), F32).at[:GLA_GATE_RANK].set(gla_w_gate_up[l])
        x2 = token_mixing(x2, B, S, cos, sin, relayout_w_in(w_in, l), wg_pad,
                          gla_b_gate_up[l].reshape(1, GLA_KEY), gla_norm_g[l].reshape(1, GLA_DV),
                          swa_sinks[l], cast_bf16(w_out, l), g[1], b[1])
        x2 = ffn_sublayer(x2, cast_gate_up(ffn2_w_gu, l), cast_bf16(ffn2_w_down, l), g[2], b[2])
    return x2.reshape(B, S, D)
```
